```python
import math
import jax, jax.numpy as jnp
from jax import lax
import numpy as np

D_MODEL = 1024
BATCH = 1
SEQ = 16384
DEPTH = 2

HEAD_DIM = 64
ATT_WIDTH = D_MODEL // 2
ATT_HEADS = ATT_WIDTH // HEAD_DIM
SG_WIDTH = D_MODEL // 4
SG_GROUPS = 4
SG_GROUP_DIM = SG_WIDTH // SG_GROUPS
CONV_WIDTH = D_MODEL // 4
MIX_WIDTH = ATT_WIDTH + SG_WIDTH + CONV_WIDTH

Q_OFF = 0
K_OFF = Q_OFF + ATT_WIDTH
V_OFF = K_OFF + ATT_WIDTH
SGU_OFF = V_OFF + ATT_WIDTH
SGV_OFF = SGU_OFF + SG_WIDTH
CA_OFF = SGV_OFF + SG_WIDTH
CG_OFF = CA_OFF + CONV_WIDTH
IN_COLS = CG_OFF + CONV_WIDTH

MOBA_BLOCK = 256
MOBA_TOPK = 3
Q_CHUNK = 128
SG_CHUNK = 128
CONV_TAPS = 31
ROPE_THETA = 10000.0
N_EXPERTS = 16
N_EXPERT_GROUPS = 4
EXPERTS_PER_GROUP = N_EXPERTS // N_EXPERT_GROUPS
TOP_K = 2
D_EXPERT = D_MODEL // 2
LN_EPS = 1e-5
NEG = -1e30
DEEPNORM_ALPHA = (2.0 * DEPTH) ** 0.25
DEEPNORM_BETA = (8.0 * DEPTH) ** -0.25

kernel_name = "hymba_moba_gmlp_conformer_grouped_moe"


def layer_norm(x, g, b):
    xf = x.astype(jnp.float32)
    mu = jnp.mean(xf, axis=-1, keepdims=True)
    var = jnp.mean(jnp.square(xf - mu), axis=-1, keepdims=True)
    return ((xf - mu) * lax.rsqrt(var + LN_EPS) * g + b).astype(x.dtype)


def rotary(x):
    s = x.shape[1]
    half = HEAD_DIM // 2
    inv_freq = ROPE_THETA ** (-jnp.arange(half, dtype=jnp.float32) / half)
    ang = jnp.arange(s, dtype=jnp.float32)[:, None] * inv_freq[None, :]
    cos = jnp.cos(ang)[None, :, None, :]
    sin = jnp.sin(ang)[None, :, None, :]
    xf = x.astype(jnp.float32)
    x1, x2 = xf[..., :half], xf[..., half:]
    return jnp.concatenate([x1 * cos - x2 * sin, x2 * cos + x1 * sin], axis=-1).astype(x.dtype)


def moba_attention(q, k, v):
    b, s, h, dh = q.shape
    bh = b * h
    nb = -(-s // MOBA_BLOCK)
    pad = nb * MOBA_BLOCK - s
    k_top = min(MOBA_TOPK, nb)
    to_g = lambda t: t.transpose(0, 2, 1, 3).reshape(bh, s, dh)
    qg, kg, vg = to_g(q), to_g(k), to_g(v)
    kb = jnp.pad(kg, ((0, 0), (0, pad), (0, 0))).reshape(bh, nb, MOBA_BLOCK, dh)
    vb = jnp.pad(vg, ((0, 0), (0, pad), (0, 0))).reshape(bh, nb, MOBA_BLOCK, dh)
    kmean = jnp.mean(kb.astype(jnp.float32), axis=2)
    nc = s // Q_CHUNK
    qc = qg.reshape(bh, nc, Q_CHUNK, dh).transpose(1, 0, 2, 3)
    scale = HEAD_DIM ** -0.5
    g_idx = jnp.arange(bh)[:, None, None]
    blk_ids = jnp.arange(nb)
    sel_rank = jnp.arange(k_top)

    def one_chunk(args):
        qi, c = args
        q_pos = c * Q_CHUNK + jnp.arange(Q_CHUNK)
        own = (c * Q_CHUNK) // MOBA_BLOCK
        gate = jnp.einsum('gqd,gnd->gqn', qi.astype(jnp.float32), kmean)
        gate = jnp.where((blk_ids < own)[None, None, :], gate, NEG)
        _, sel = lax.top_k(gate, k_top)
        sel_ok = sel_rank < own
        k_sel = kb[g_idx, sel]
        v_sel = vb[g_idx, sel]
        s_sel = jnp.einsum('gqd,gqnkd->gqnk', qi, k_sel).astype(jnp.float32) * scale
        s_sel = jnp.where(sel_ok[None, None, :, None], s_sel, NEG).reshape(bh, Q_CHUNK, k_top * MOBA_BLOCK)
        k_own = lax.dynamic_index_in_dim(kb, own, axis=1, keepdims=False)
        v_own = lax.dynamic_index_in_dim(vb, own, axis=1, keepdims=False)
        s_own = jnp.einsum('gqd,gkd->gqk', qi, k_own).astype(jnp.float32) * scale
        k_pos = own * MOBA_BLOCK + jnp.arange(MOBA_BLOCK)
        s_own = jnp.where((k_pos[None, :] <= q_pos[:, None])[None], s_own, NEG)
        p = jax.nn.softmax(jnp.concatenate([s_sel, s_own], axis=-1), axis=-1)
        p_sel = p[..., :k_top * MOBA_BLOCK].reshape(bh, Q_CHUNK, k_top, MOBA_BLOCK).astype(v.dtype)
        p_own = p[..., k_top * MOBA_BLOCK:].astype(v.dtype)
        return (jnp.einsum('gqnk,gqnkd->gqd', p_sel, v_sel)
                + jnp.einsum('gqk,gkd->gqd', p_own, v_own))

    out = lax.map(one_chunk, (qc, jnp.arange(nc)))
    out = out.transpose(1, 0, 2, 3).reshape(b, h, s, dh).transpose(0, 2, 1, 3)
    return out.reshape(b, s, h * dh)


def spatial_gating(u, v, ln_g, ln_b, w_s, b_s):
    b, s, _ = v.shape
    v = layer_norm(v, ln_g, ln_b)
    vc = v.reshape(b, s // SG_CHUNK, SG_CHUNK, SG_GROUPS, SG_GROUP_DIM)
    causal = jnp.tril(jnp.ones((SG_CHUNK, SG_CHUNK), dtype=bool))
    w = jnp.where(causal[None], w_s, 0.0)
    mixed = jnp.einsum('gts,bcsgd->bctgd', w, vc) + b_s.T[None, None, :, :, None]
    return u * mixed.reshape(b, s, SG_WIDTH)


def conv_module(a, g, w_dw, b_dw, ln_g, ln_b):
    h = a * jax.nn.sigmoid(g)
    h = lax.conv_general_dilated(h, w_dw[:, None, :], window_strides=(1,),
                                 padding=((CONV_TAPS - 1, 0),),
                                 dimension_numbers=('NWC', 'WIO', 'NWC'),
                                 feature_group_count=CONV_WIDTH) + b_dw
    h = layer_norm(h, ln_g, ln_b)
    return jax.nn.silu(h)


def hybrid_mixer(h, w_in, w_out, sg_ln_g, sg_ln_b, sg_w, sg_b, conv_w, conv_b, conv_ln_g, conv_ln_b):
    b, s, _ = h.shape
    p = h @ w_in
    q = rotary(p[..., Q_OFF:K_OFF].reshape(b, s, ATT_HEADS, HEAD_DIM))
    k = rotary(p[..., K_OFF:V_OFF].reshape(b, s, ATT_HEADS, HEAD_DIM))
    v = p[..., V_OFF:SGU_OFF].reshape(b, s, ATT_HEADS, HEAD_DIM)
    y_att = moba_attention(q, k, v)
    y_sg = spatial_gating(jax.nn.gelu(p[..., SGU_OFF:SGV_OFF]), jax.nn.gelu(p[..., SGV_OFF:CA_OFF]),
                          sg_ln_g, sg_ln_b, sg_w, sg_b)
    y_cv = conv_module(p[..., CA_OFF:CG_OFF], p[..., CG_OFF:IN_COLS], conv_w, conv_b, conv_ln_g, conv_ln_b)
    return jnp.concatenate([y_att, y_sg, y_cv], axis=-1) @ w_out


def grouped_moe(h, w_router, router_bias, w_gate, w_up, w_down):
    b, s, d = h.shape
    hf = h.reshape(b * s, d)
    n = hf.shape[0]
    scores = jax.nn.sigmoid((hf @ w_router).astype(jnp.float32))
    biased = (scores + router_bias.astype(jnp.float32)).reshape(n, N_EXPERT_GROUPS, EXPERTS_PER_GROUP)
    grp_score = jnp.sum(lax.top_k(biased, TOP_K)[0], axis=-1)
    grp = jnp.argmax(grp_score, axis=-1)
    in_grp = jnp.take_along_axis(biased, grp[:, None, None], axis=1)[:, 0]
    _, local = lax.top_k(in_grp, TOP_K)
    expert_ids = grp[:, None] * EXPERTS_PER_GROUP + local
    w_sel = jnp.take_along_axis(scores, expert_ids, axis=1)
    w_sel = w_sel / jnp.sum(w_sel, axis=-1, keepdims=True)
    gates = jnp.sum(jax.nn.one_hot(expert_ids, N_EXPERTS, dtype=jnp.float32) * w_sel[..., None], axis=1)
    gates = gates.astype(hf.dtype)
    out = jnp.zeros_like(hf)
    for e in range(N_EXPERTS):
        y = (jax.nn.silu(hf @ w_gate[e]) * (hf @ w_up[e])) @ w_down[e]
        out = out + gates[:, e:e + 1] * y
    return out.reshape(b, s, d)


def setup_inputs(seed: int = 0) -> dict:
    key = jax.random.key(seed)
    ks = jax.random.split(key, 24)
    f32 = jnp.float32
    nrm = lambda k, shape, sc: jax.random.normal(k, shape, f32) * sc
    gain = lambda k, shape: 1.0 + 0.02 * jax.random.normal(k, shape, f32)
    return {
        "x": jax.random.normal(ks[0], (BATCH, SEQ, D_MODEL), f32),
        "ln_in_g": gain(ks[1], (D_MODEL,)),
        "ln_in_b": nrm(ks[2], (D_MODEL,), 0.02),
        "w_in": nrm(ks[3], (DEPTH, D_MODEL, IN_COLS), D_MODEL ** -0.5),
        "w_out": nrm(ks[4], (DEPTH, MIX_WIDTH, D_MODEL), MIX_WIDTH ** -0.5 * DEEPNORM_BETA),
        "sg_ln_g": gain(ks[5], (DEPTH, SG_WIDTH)),
        "sg_ln_b": nrm(ks[6], (DEPTH, SG_WIDTH), 0.02),
        "sg_w": nrm(ks[7], (DEPTH, SG_GROUPS, SG_CHUNK, SG_CHUNK), SG_CHUNK ** -0.5),
        "sg_b": gain(ks[8], (DEPTH, SG_GROUPS, SG_CHUNK)),
        "conv_w": nrm(ks[9], (DEPTH, CONV_TAPS, CONV_WIDTH), CONV_TAPS ** -0.5),
        "conv_b": nrm(ks[10], (DEPTH, CONV_WIDTH), 0.02),
        "conv_ln_g": gain(ks[11], (DEPTH, CONV_WIDTH)),
        "conv_ln_b": nrm(ks[12], (DEPTH, CONV_WIDTH), 0.02),
        "ln_mix_g": gain(ks[13], (DEPTH, D_MODEL)),
        "ln_mix_b": nrm(ks[14], (DEPTH, D_MODEL), 0.02),
        "w_router": nrm(ks[15], (D_MODEL, N_EXPERTS), D_MODEL ** -0.5),
        "router_bias": nrm(ks[16], (N_EXPERTS,), 0.01),
        "w_gate": nrm(ks[17], (DEPTH, N_EXPERTS, D_MODEL, D_EXPERT), D_MODEL ** -0.5),
        "w_up": nrm(ks[18], (DEPTH, N_EXPERTS, D_MODEL, D_EXPERT), D_MODEL ** -0.5),
        "w_down": nrm(ks[19], (DEPTH, N_EXPERTS, D_EXPERT, D_MODEL), D_EXPERT ** -0.5 * DEEPNORM_BETA),
        "ln_ffn_g": gain(ks[20], (DEPTH, D_MODEL)),
        "ln_ffn_b": nrm(ks[21], (DEPTH, D_MODEL), 0.02),
    }


def reference(x, ln_in_g, ln_in_b, w_in, w_out, sg_ln_g, sg_ln_b, sg_w, sg_b, conv_w, conv_b,
              conv_ln_g, conv_ln_b, ln_mix_g, ln_mix_b, w_router, router_bias, w_gate, w_up, w_down,
              ln_ffn_g, ln_ffn_b):
    h = layer_norm(x, ln_in_g, ln_in_b)
    for l in range(DEPTH):
        mix = hybrid_mixer(h, w_in[l], w_out[l], sg_ln_g[l], sg_ln_b[l], sg_w[l], sg_b[l],
                           conv_w[l], conv_b[l], conv_ln_g[l], conv_ln_b[l])
        h = layer_norm(DEEPNORM_ALPHA * h + mix, ln_mix_g[l], ln_mix_b[l])
        ffn = grouped_moe(h, w_router, router_bias, w_gate[l], w_up[l], w_down[l])
        h = layer_norm(DEEPNORM_ALPHA * h + ffn, ln_ffn_g[l], ln_ffn_b[l])
    return h
```

```python
import functools
import math

import jax
import jax.numpy as jnp
from jax import lax
from jax.experimental import pallas as pl
from jax.experimental.pallas import tpu as pltpu

D_MODEL = 1024
SEQ = 16384
DEPTH = 2
HEAD_DIM = 64
ATT_WIDTH = 512
ATT_HEADS = 8
SG_WIDTH = 256
SG_GROUPS = 4
SG_GROUP_DIM = 64
CONV_WIDTH = 256
K_OFF = 512
V_OFF = 1024
SGU_OFF = 1536
SGV_OFF = 1792
CA_OFF = 2048
CG_OFF = 2304
IN_COLS = 2560
MOBA_BLOCK = 256
MOBA_TOPK = 3
N_BLOCKS = SEQ // MOBA_BLOCK
SG_CHUNK = 128
CONV_TAPS = 31
ROPE_THETA = 10000.0
N_EXPERTS = 16
N_EXPERT_GROUPS = 4
EXPERTS_PER_GROUP = 4
D_EXPERT = 512
LN_EPS = 1e-5
NEG = -1e30
DEEPNORM_ALPHA = (2.0 * DEPTH) ** 0.25

LANES = 128
SUBLANES = 8
ROW_TILE = MOBA_BLOCK
CONV_HALO = 32
MOE_ROW_TILE = 1024
VMEM_LIMIT = 48 * 1024 * 1024

F32 = jnp.float32
BF16 = jnp.bfloat16


def _dot(a, b):
    return jnp.dot(a, b, preferred_element_type=F32)


def _dot_nt(a, b):
    return lax.dot_general(a, b, (((1,), (1,)), ((), ())), preferred_element_type=F32)


def _layer_norm(x, g, b):
    mu = jnp.mean(x, axis=-1, keepdims=True)
    xc = x - mu
    var = jnp.mean(xc * xc, axis=-1, keepdims=True)
    return xc * lax.rsqrt(var + LN_EPS) * g + b


def _split_bf16(x):
    hi = x.astype(BF16)
    lo = (x - hi.astype(F32)).astype(BF16)
    return hi, lo


def _front_kernel(apply_ln, *refs):
    if apply_ln:
        (x_ref, lng_ref, lnb_ref, w_ref, cos_ref, sin_ref, sglng_ref, sglnb_ref, sgw_ref, sgb_ref,
         cw_ref, cb_ref, clng_ref, clnb_ref,
         h_ref, q_ref, kx_ref, v_ref, km_ref, ysg_ref, ycv_ref, ext_ref) = refs
    else:
        (x_ref, w_ref, cos_ref, sin_ref, sglng_ref, sglnb_ref, sgw_ref, sgb_ref,
         cw_ref, cb_ref, clng_ref, clnb_ref,
         q_ref, kx_ref, v_ref, km_ref, ysg_ref, ycv_ref, ext_ref) = refs
    i = pl.program_id(0)
    h = x_ref[...]
    if apply_ln:
        h = _layer_norm(h, lng_ref[...], lnb_ref[...])
        h_ref[...] = h
    p = _dot(h.astype(BF16), w_ref[...])

    lane = lax.broadcasted_iota(jnp.int32, (ROW_TILE, LANES), 1)
    first_half = (lane % HEAD_DIM) < (HEAD_DIM // 2)
    low_head = lane < HEAD_DIM
    cos = cos_ref[...]
    sin = sin_ref[...]

    def rot(x):
        swapped = jnp.where(first_half, pltpu.roll(x, LANES - HEAD_DIM // 2, 1), pltpu.roll(x, HEAD_DIM // 2, 1))
        return x * cos + swapped * sin

    scale = HEAD_DIM ** -0.5
    for j in range(ATT_WIDTH // LANES):
        sl = slice(j * LANES, (j + 1) * LANES)
        q_ref[:, sl] = (rot(p[:, sl]) * scale).astype(BF16)
        kr = rot(p[:, K_OFF + j * LANES:K_OFF + (j + 1) * LANES])
        km_ref[0, :, sl] = jnp.broadcast_to(jnp.mean(kr, axis=0, keepdims=True), (SUBLANES, LANES))
        kx_ref[2 * j] = jnp.where(low_head, kr, (lane - HEAD_DIM == i).astype(F32)).astype(BF16)
        kx_ref[2 * j + 1] = jnp.where(low_head, (lane == i).astype(F32), kr).astype(BF16)
    v_ref[...] = p[:, V_OFF:SGU_OFF].astype(BF16)

    u = jax.nn.gelu(p[:, SGU_OFF:SGV_OFF])
    vv = _layer_norm(jax.nn.gelu(p[:, SGV_OFF:CA_OFF]), sglng_ref[...], sglnb_ref[...]).astype(BF16)
    r_i = lax.broadcasted_iota(jnp.int32, (SG_CHUNK, SG_CHUNK), 0)
    c_i = lax.broadcasted_iota(jnp.int32, (SG_CHUNK, SG_CHUNK), 1)
    lane_sg = lax.broadcasted_iota(jnp.int32, (SG_CHUNK, SG_WIDTH), 1)
    for c in range(ROW_TILE // SG_CHUNK):
        rows = slice(c * SG_CHUNK, (c + 1) * SG_CHUNK)
        mixed = sgb_ref[...]
        for g in range(SG_GROUPS):
            wg = jnp.where(c_i <= r_i, sgw_ref[g], 0.0).astype(BF16)
            full = _dot(wg, vv[rows])
            mixed = mixed + jnp.where(lane_sg // SG_GROUP_DIM == g, full, 0.0)
        ysg_ref[rows, :] = (u[rows] * mixed).astype(BF16)

    @pl.when(i == 0)
    def _():
        ext_ref[0:CONV_HALO, :] = jnp.zeros((CONV_HALO, CONV_WIDTH), F32)

    ext_ref[CONV_HALO:, :] = p[:, CA_OFF:CG_OFF] * jax.nn.sigmoid(p[:, CG_OFF:IN_COLS])
    acc = jnp.zeros((ROW_TILE, CONV_WIDTH), F32) + cb_ref[...]
    first = CONV_HALO - (CONV_TAPS - 1)
    for j in range(CONV_TAPS):
        acc = acc + cw_ref[j:j + 1, :] * ext_ref[first + j:first + j + ROW_TILE, :]
    ext_ref[0:CONV_HALO, :] = ext_ref[ROW_TILE:ROW_TILE + CONV_HALO, :]
    hc = _layer_norm(acc, clng_ref[...], clnb_ref[...])
    ycv_ref[...] = (hc * jax.nn.sigmoid(hc)).astype(BF16)


def _front(apply_ln, x, ln_g, ln_b, w_in, cos, sin, sg_ln_g, sg_ln_b, sg_w, sg_bias, conv_w, conv_b,
           conv_ln_g, conv_ln_b):
    n_tiles = SEQ // ROW_TILE
    row = lambda w: pl.BlockSpec((ROW_TILE, w), lambda i: (i, 0))
    full = lambda shape: pl.BlockSpec(shape, lambda i: (0,) * len(shape))
    in_specs = [row(D_MODEL)]
    args = [x]
    if apply_ln:
        in_specs += [full((1, D_MODEL)), full((1, D_MODEL))]
        args += [ln_g, ln_b]
    in_specs += [full((D_MODEL, IN_COLS)), row(LANES), row(LANES), full((1, SG_WIDTH)), full((1, SG_WIDTH)),
                 full((SG_GROUPS, SG_CHUNK, SG_CHUNK)), full((SG_CHUNK, SG_WIDTH)),
                 full((CONV_TAPS, CONV_WIDTH)), full((1, CONV_WIDTH)), full((1, CONV_WIDTH)),
                 full((1, CONV_WIDTH))]
    args += [w_in, cos, sin, sg_ln_g, sg_ln_b, sg_w, sg_bias, conv_w, conv_b, conv_ln_g, conv_ln_b]
    out_shape = [jax.ShapeDtypeStruct((SEQ, ATT_WIDTH), BF16),
                 jax.ShapeDtypeStruct((ATT_HEADS, SEQ, LANES), BF16),
                 jax.ShapeDtypeStruct((SEQ, ATT_WIDTH), BF16),
                 jax.ShapeDtypeStruct((N_BLOCKS, SUBLANES, ATT_WIDTH), F32),
                 jax.ShapeDtypeStruct((SEQ, SG_WIDTH), BF16),
                 jax.ShapeDtypeStruct((SEQ, CONV_WIDTH), BF16)]
    out_specs = [row(ATT_WIDTH), pl.BlockSpec((ATT_HEADS, ROW_TILE, LANES), lambda i: (0, i, 0)),
                 row(ATT_WIDTH), pl.BlockSpec((1, SUBLANES, ATT_WIDTH), lambda i: (i, 0, 0)),
                 row(SG_WIDTH), row(CONV_WIDTH)]
    if apply_ln:
        out_shape = [jax.ShapeDtypeStruct((SEQ, D_MODEL), F32)] + out_shape
        out_specs = [row(D_MODEL)] + out_specs
    return pl.pallas_call(
        functools.partial(_front_kernel, apply_ln),
        grid=(n_tiles,),
        in_specs=in_specs,
        out_specs=out_specs,
        out_shape=out_shape,
        scratch_shapes=[pltpu.VMEM((CONV_HALO + ROW_TILE, CONV_WIDTH), F32)],
        compiler_params=pltpu.CompilerParams(dimension_semantics=("arbitrary",), vmem_limit_bytes=VMEM_LIMIT),
        name="front_ln" if apply_ln else "front",
    )(*args)


def _attn_kernel(q_ref, kx_ref, v_ref, km_ref, o_ref):
    own = pl.program_id(1)
    q = q_ref[...]
    lane = lax.broadcasted_iota(jnp.int32, (ROW_TILE, LANES), 1)
    low = lane < HEAD_DIM
    blk = lane % HEAD_DIM

    km = km_ref[...]
    km_lane = lax.broadcasted_iota(jnp.int32, (N_BLOCKS, LANES), 1)
    km_rows = jnp.concatenate([jnp.where(km_lane >= HEAD_DIM, km, 0.0), jnp.where(km_lane < HEAD_DIM, km, 0.0)],
                              axis=0)
    km_hi, km_lo = _split_bf16(km_rows)
    gate = _dot_nt(q, km_hi) + _dot_nt(q, km_lo)
    gate = jnp.where(blk < own, gate, NEG)

    sel = blk == own
    lane_f = lane.astype(F32)
    own_v = jnp.broadcast_to(own, (ROW_TILE, LANES))
    for r in range(MOBA_TOPK):
        m_lo = jnp.max(jnp.where(low, gate, -jnp.inf), axis=1, keepdims=True)
        m_hi = jnp.max(jnp.where(low, -jnp.inf, gate), axis=1, keepdims=True)
        cand = jnp.where(gate == jnp.where(low, m_lo, m_hi), lane_f, float(LANES))
        i_lo = jnp.min(jnp.where(low, cand, float(LANES)), axis=1, keepdims=True)
        i_hi = jnp.min(jnp.where(low, float(LANES), cand), axis=1, keepdims=True)
        pick = lane_f == jnp.where(low, i_lo, i_hi)
        sel = sel | (pick & (own_v > r))
        gate = jnp.where(pick, -jnp.inf, gate)
    bias = jnp.where(sel, 0.0, NEG).astype(BF16)

    row_i = lax.broadcasted_iota(jnp.int32, (ROW_TILE, MOBA_BLOCK), 0)
    col_i = lax.broadcasted_iota(jnp.int32, (ROW_TILE, MOBA_BLOCK), 1)
    start_own = pl.multiple_of(own * MOBA_BLOCK, MOBA_BLOCK)

    outs = []
    for par in range(2):
        qx = jnp.where(low, q, bias) if par == 0 else jnp.where(low, bias, q)

        s = _dot_nt(qx, kx_ref[par, pl.ds(start_own, MOBA_BLOCK), :])
        s = jnp.where(col_i <= row_i, s, NEG)
        m0 = jnp.max(s, axis=1, keepdims=True)
        p0 = jnp.exp(s - m0)
        l0 = jnp.sum(p0, axis=1, keepdims=True)
        acc0 = _dot(p0.astype(BF16), v_ref[pl.ds(start_own, MOBA_BLOCK), :])

        def body(n, carry, qx=qx, par=par):
            m, l, acc = carry
            start = pl.multiple_of(n * MOBA_BLOCK, MOBA_BLOCK)
            s = _dot_nt(qx, kx_ref[par, pl.ds(start, MOBA_BLOCK), :])
            m_new = jnp.maximum(m, jnp.max(s, axis=1, keepdims=True))
            a = jnp.exp(m - m_new)
            p = jnp.exp(s - m_new)
            l = a * l + jnp.sum(p, axis=1, keepdims=True)
            acc = a * acc + _dot(p.astype(BF16), v_ref[pl.ds(start, MOBA_BLOCK), :])
            return m_new, l, acc

        _, l, acc = lax.fori_loop(0, own, body, (m0, l0, acc0))
        outs.append(acc / l)
    o_ref[...] = jnp.where(low, outs[0], outs[1]).astype(BF16)


def _attn(q, kx, v, kmean):
    n_pairs = ATT_WIDTH // LANES
    return pl.pallas_call(
        _attn_kernel,
        grid=(n_pairs, SEQ // ROW_TILE),
        in_specs=[pl.BlockSpec((ROW_TILE, LANES), lambda j, t: (t, j)),
                  pl.BlockSpec((2, SEQ, LANES), lambda j, t: (j, 0, 0)),
                  pl.BlockSpec((SEQ, LANES), lambda j, t: (0, j)),
                  pl.BlockSpec((N_BLOCKS, LANES), lambda j, t: (0, j))],
        out_specs=pl.BlockSpec((ROW_TILE, LANES), lambda j, t: (t, j)),
        out_shape=jax.ShapeDtypeStruct((SEQ, ATT_WIDTH), BF16),
        compiler_params=pltpu.CompilerParams(dimension_semantics=("arbitrary", "arbitrary"),
                                             vmem_limit_bytes=VMEM_LIMIT),
        name="attn",
    )(q, kx, v, kmean)


def _outproj_kernel(h_ref, ya_ref, ys_ref, yc_ref, wo_ref, g_ref, b_ref, wr_ref, rb_ref, h1_ref, gates_ref):
    mix = (_dot(ya_ref[...], wo_ref[0:ATT_WIDTH, :])
           + _dot(ys_ref[...], wo_ref[ATT_WIDTH:ATT_WIDTH + SG_WIDTH, :])
           + _dot(yc_ref[...], wo_ref[ATT_WIDTH + SG_WIDTH:, :]))
    h1 = _layer_norm(DEEPNORM_ALPHA * h_ref[...] + mix, g_ref[...], b_ref[...])
    h1_ref[...] = h1

    w_hi, w_lo = _split_bf16(wr_ref[...])
    h_hi, h_lo = _split_bf16(h1)
    both = _dot_nt(jnp.concatenate([w_hi, w_lo], axis=0), h_hi)
    logits = both[0:N_EXPERTS] + both[N_EXPERTS:] + _dot_nt(w_hi, h_lo)
    scores = jax.nn.sigmoid(logits)
    biased = scores + rb_ref[...]
    sc = [scores[e:e + 1, :] for e in range(N_EXPERTS)]
    bi = [biased[e:e + 1, :] for e in range(N_EXPERTS)]

    best = None
    for g in range(N_EXPERT_GROUPS):
        a0, a1, a2, a3 = bi[4 * g:4 * g + 4]
        hi01, lo01 = jnp.maximum(a0, a1), jnp.minimum(a0, a1)
        hi23, lo23 = jnp.maximum(a2, a3), jnp.minimum(a2, a3)
        top1 = jnp.maximum(hi01, hi23)
        top2 = jnp.maximum(jnp.minimum(hi01, hi23), jnp.maximum(lo01, lo23))
        gs = top1 + top2
        if best is None:
            best, grp = gs, jnp.zeros_like(gs, dtype=jnp.int32)
        else:
            upd = gs > best
            best = jnp.where(upd, gs, best)
            grp = jnp.where(upd, g, grp)

    def in_group(vals, k):
        out = vals[k]
        for g in range(1, N_EXPERT_GROUPS):
            out = jnp.where(grp == g, vals[4 * g + k], out)
        return out

    vb = [in_group(bi, k) for k in range(EXPERTS_PER_GROUP)]
    vs = [in_group(sc, k) for k in range(EXPERTS_PER_GROUP)]
    m1, i1, s1 = vb[0], jnp.zeros_like(grp), vs[0]
    for k in range(1, EXPERTS_PER_GROUP):
        upd = vb[k] > m1
        m1 = jnp.where(upd, vb[k], m1)
        i1 = jnp.where(upd, k, i1)
        s1 = jnp.where(upd, vs[k], s1)
    m2 = jnp.full_like(m1, -jnp.inf)
    i2 = jnp.zeros_like(grp)
    s2 = jnp.zeros_like(s1)
    for k in range(EXPERTS_PER_GROUP):
        upd = (i1 != k) & (vb[k] > m2)
        m2 = jnp.where(upd, vb[k], m2)
        i2 = jnp.where(upd, k, i2)
        s2 = jnp.where(upd, vs[k], s2)
    tot = s1 + s2
    e1 = grp * EXPERTS_PER_GROUP + i1
    e2 = grp * EXPERTS_PER_GROUP + i2
    e_idx = lax.broadcasted_iota(jnp.int32, (N_EXPERTS, ROW_TILE), 0)
    gates_ref[...] = jnp.where(e_idx == e1, s1 / tot, 0.0) + jnp.where(e_idx == e2, s2 / tot, 0.0)


def _outproj(h, y_att, y_sg, y_cv, w_out, ln_g, ln_b, w_router_t, router_bias):
    row = lambda w: pl.BlockSpec((ROW_TILE, w), lambda i: (i, 0))
    full = lambda shape: pl.BlockSpec(shape, lambda i: (0,) * len(shape))
    return pl.pallas_call(
        _outproj_kernel,
        grid=(SEQ // ROW_TILE,),
        in_specs=[row(D_MODEL), row(ATT_WIDTH), row(SG_WIDTH), row(CONV_WIDTH), full((D_MODEL, D_MODEL)),
                  full((1, D_MODEL)), full((1, D_MODEL)), full((N_EXPERTS, D_MODEL)), full((N_EXPERTS, 1))],
        out_specs=[row(D_MODEL), pl.BlockSpec((N_EXPERTS, ROW_TILE), lambda i: (0, i))],
        out_shape=[jax.ShapeDtypeStruct((SEQ, D_MODEL), F32), jax.ShapeDtypeStruct((N_EXPERTS, SEQ), F32)],
        compiler_params=pltpu.CompilerParams(dimension_semantics=("arbitrary",), vmem_limit_bytes=VMEM_LIMIT),
        name="outproj",
    )(h, y_att, y_sg, y_cv, w_out, ln_g, ln_b, w_router_t, router_bias)


def _moe_kernel(h_ref, gates_ref, wg_ref, wu_ref, wd_ref, g_ref, b_ref, o_ref, acc_ref, xb_ref):
    e = pl.program_id(1)

    @pl.when(e == 0)
    def _():
        acc_ref[...] = jnp.zeros_like(acc_ref)
        xb_ref[...] = h_ref[...].astype(BF16)

    x = xb_ref[...]
    a = jax.nn.silu(_dot(x, wg_ref[0])) * _dot(x, wu_ref[0])
    y = _dot(a.astype(BF16), wd_ref[0])
    lane = lax.broadcasted_iota(jnp.int32, (MOE_ROW_TILE, N_EXPERTS), 1)
    gate = jnp.sum(jnp.where(lane == e, gates_ref[...], 0.0), axis=1, keepdims=True)
    acc_ref[...] += gate * y

    @pl.when(e == N_EXPERTS - 1)
    def _():
        o_ref[...] = _layer_norm(DEEPNORM_ALPHA * h_ref[...] + acc_ref[...], g_ref[...], b_ref[...])


def _moe(h, gates, w_gate, w_up, w_down, ln_g, ln_b):
    return pl.pallas_call(
        _moe_kernel,
        grid=(SEQ // MOE_ROW_TILE, N_EXPERTS),
        in_specs=[pl.BlockSpec((MOE_ROW_TILE, D_MODEL), lambda i, e: (i, 0)),
                  pl.BlockSpec((MOE_ROW_TILE, N_EXPERTS), lambda i, e: (i, 0)),
                  pl.BlockSpec((1, D_MODEL, D_EXPERT), lambda i, e: (e, 0, 0)),
                  pl.BlockSpec((1, D_MODEL, D_EXPERT), lambda i, e: (e, 0, 0)),
                  pl.BlockSpec((1, D_EXPERT, D_MODEL), lambda i, e: (e, 0, 0)),
                  pl.BlockSpec((1, D_MODEL), lambda i, e: (0, 0)),
                  pl.BlockSpec((1, D_MODEL), lambda i, e: (0, 0))],
        out_specs=pl.BlockSpec((MOE_ROW_TILE, D_MODEL), lambda i, e: (i, 0)),
        out_shape=jax.ShapeDtypeStruct((SEQ, D_MODEL), F32),
        scratch_shapes=[pltpu.VMEM((MOE_ROW_TILE, D_MODEL), F32), pltpu.VMEM((MOE_ROW_TILE, D_MODEL), BF16)],
        compiler_params=pltpu.CompilerParams(dimension_semantics=("arbitrary", "arbitrary"),
                                             vmem_limit_bytes=VMEM_LIMIT),
        name="moe",
    )(h, gates, w_gate, w_up, w_down, ln_g, ln_b)


def _rope_tables():
    half = HEAD_DIM // 2
    inv_freq = ROPE_THETA ** (-jnp.arange(half, dtype=F32) / half)
    ang = jnp.arange(SEQ, dtype=F32)[:, None] * inv_freq[None, :]
    cos, sin = jnp.cos(ang), jnp.sin(ang)
    reps = LANES // HEAD_DIM
    return jnp.tile(jnp.concatenate([cos, cos], axis=1), (1, reps)), jnp.tile(jnp.concatenate([-sin, sin], axis=1), (1, reps))


def kernel(x, ln_in_g, ln_in_b, w_in, w_out, sg_ln_g, sg_ln_b, sg_w, sg_b, conv_w, conv_b, conv_ln_g, conv_ln_b, ln_mix_g, ln_mix_b, w_router, router_bias, w_gate, w_up, w_down, ln_ffn_g, ln_ffn_b):
    assert x.shape == (1, SEQ, D_MODEL) and w_in.shape == (DEPTH, D_MODEL, IN_COLS)
    cos, sin = _rope_tables()
    row = lambda a: a.reshape(1, -1)
    w_router_t = w_router.T
    rb = router_bias.reshape(N_EXPERTS, 1)
    h = x.reshape(SEQ, D_MODEL)
    for l in range(DEPTH):
        sg_bias = jnp.repeat(sg_b[l].T, SG_GROUP_DIM, axis=1)
        args = (w_in[l].astype(BF16), cos, sin, row(sg_ln_g[l]), row(sg_ln_b[l]), sg_w[l], sg_bias,
                conv_w[l], row(conv_b[l]), row(conv_ln_g[l]), row(conv_ln_b[l]))
        if l == 0:
            h, q, kx, v, kmean, y_sg, y_cv = _front(True, h, row(ln_in_g), row(ln_in_b), *args)
        else:
            q, kx, v, kmean, y_sg, y_cv = _front(False, h, None, None, *args)
        y_att = _attn(q, kx, v, kmean[:, 0, :])
        h, gates_t = _outproj(h, y_att, y_sg, y_cv, w_out[l].astype(BF16), row(ln_mix_g[l]), row(ln_mix_b[l]),
                              w_router_t, rb)
        h = _moe(h, gates_t.T, w_gate[l].astype(BF16), w_up[l].astype(BF16), w_down[l].astype(BF16),
                 row(ln_ffn_g[l]), row(ln_ffn_b[l]))
    return h.reshape(1, SEQ, D_MODEL)
```

```python
import functools
import math

import jax
import jax.numpy as jnp
from jax import lax
from jax.experimental import pallas as pl
from jax.experimental.pallas import tpu as pltpu

D_MODEL = 1024
SEQ = 16384
DEPTH = 2
HEAD_DIM = 64
ATT_WIDTH = 512
ATT_HEADS = 8
SG_WIDTH = 256
SG_GROUPS = 4
SG_GROUP_DIM = 64
CONV_WIDTH = 256
K_OFF = 512
V_OFF = 1024
SGU_OFF = 1536
SGV_OFF = 1792
CA_OFF = 2048
CG_OFF = 2304
IN_COLS = 2560
P_SGU = SGU_OFF - ATT_WIDTH
P_SGV = SGV_OFF - ATT_WIDTH
P_CA = CA_OFF - ATT_WIDTH
P_CG = CG_OFF - ATT_WIDTH
P_COLS = IN_COLS - ATT_WIDTH
MOBA_BLOCK = 256
MOBA_TOPK = 3
N_BLOCKS = SEQ // MOBA_BLOCK
SG_CHUNK = 128
CONV_TAPS = 31
ROPE_THETA = 10000.0
N_EXPERTS = 16
N_EXPERT_GROUPS = 4
EXPERTS_PER_GROUP = 4
D_EXPERT = 512
LN_EPS = 1e-5
NEG = -1e30
DEEPNORM_ALPHA = (2.0 * DEPTH) ** 0.25

LANES = 128
SUBLANES = 8
ROW_TILE = MOBA_BLOCK
CONV_HALO = 32
LOG2_E = math.log2(math.e)
VT_ROWS = HEAD_DIM + 16
ATT_GROUP_SHIFT = 2
ATT_GROUP = 1 << ATT_GROUP_SHIFT
ATT_GROUP_KEYS = ATT_GROUP * MOBA_BLOCK
MOE_ROW_TILE = 1024
VMEM_LIMIT = 48 * 1024 * 1024

F32 = jnp.float32
BF16 = jnp.bfloat16


def _dot(a, b):
    return jnp.dot(a, b, preferred_element_type=F32)


def _dot_nt(a, b):
    return lax.dot_general(a, b, (((1,), (1,)), ((), ())), preferred_element_type=F32)


def _layer_norm(x, g, b):
    mu = jnp.mean(x, axis=-1, keepdims=True)
    xc = x - mu
    var = jnp.mean(xc * xc, axis=-1, keepdims=True)
    return xc * lax.rsqrt(var + LN_EPS) * g + b


def _split_bf16(x):
    hi = x.astype(BF16)
    lo = (x - hi.astype(F32)).astype(BF16)
    return hi, lo


def _front_kernel(apply_ln, *refs):
    if apply_ln:
        (x_ref, lng_ref, lnb_ref, w_ref, wvt_ref, cos_ref, sin_ref, sglng_ref, sglnb_ref, sgw_ref, sgb_ref,
         cw_ref, cb_ref, clng_ref, clnb_ref,
         h_ref, q_ref, kx_ref, vt_ref, km_ref, ysg_ref, ycv_ref, ext_ref) = refs
    else:
        (x_ref, w_ref, wvt_ref, cos_ref, sin_ref, sglng_ref, sglnb_ref, sgw_ref, sgb_ref,
         cw_ref, cb_ref, clng_ref, clnb_ref,
         q_ref, kx_ref, vt_ref, km_ref, ysg_ref, ycv_ref, ext_ref) = refs
    i = pl.program_id(0)
    h = x_ref[...]
    if apply_ln:
        h = _layer_norm(h, lng_ref[...], lnb_ref[...])
        h_ref[...] = h
    hb = h.astype(BF16)
    p = _dot(hb, w_ref[...])
    vt_ref[:, 0:HEAD_DIM, :] = _dot_nt(wvt_ref[...], hb).reshape(ATT_HEADS, HEAD_DIM, ROW_TILE).astype(BF16)
    vt_ref[:, HEAD_DIM:, :] = jnp.ones((ATT_HEADS, VT_ROWS - HEAD_DIM, ROW_TILE), BF16)

    lane = lax.broadcasted_iota(jnp.int32, (ROW_TILE, LANES), 1)
    first_half = (lane % HEAD_DIM) < (HEAD_DIM // 2)
    low_head = lane < HEAD_DIM
    cos = cos_ref[...]
    sin = sin_ref[...]

    def rot(x):
        swapped = jnp.where(first_half, pltpu.roll(x, LANES - HEAD_DIM // 2, 1), pltpu.roll(x, HEAD_DIM // 2, 1))
        return x * cos + swapped * sin

    scale = HEAD_DIM ** -0.5 * LOG2_E
    for j in range(ATT_WIDTH // LANES):
        sl = slice(j * LANES, (j + 1) * LANES)
        q_ref[:, sl] = (rot(p[:, sl]) * scale).astype(BF16)
        kr = rot(p[:, K_OFF + j * LANES:K_OFF + (j + 1) * LANES])
        km_ref[0, :, sl] = jnp.broadcast_to(jnp.mean(kr, axis=0, keepdims=True), (SUBLANES, LANES))
        kx_ref[2 * j] = jnp.where(low_head, kr, (lane - HEAD_DIM == i).astype(F32)).astype(BF16)
        kx_ref[2 * j + 1] = jnp.where(low_head, (lane == i).astype(F32), kr).astype(BF16)

    u = jax.nn.gelu(p[:, P_SGU:P_SGV])
    vv = _layer_norm(jax.nn.gelu(p[:, P_SGV:P_CA]), sglng_ref[...], sglnb_ref[...]).astype(BF16)
    r_i = lax.broadcasted_iota(jnp.int32, (SG_CHUNK, SG_CHUNK), 0)
    c_i = lax.broadcasted_iota(jnp.int32, (SG_CHUNK, SG_CHUNK), 1)
    lane_sg = lax.broadcasted_iota(jnp.int32, (SG_CHUNK, SG_WIDTH), 1)
    for c in range(ROW_TILE // SG_CHUNK):
        rows = slice(c * SG_CHUNK, (c + 1) * SG_CHUNK)
        mixed = sgb_ref[...]
        for g in range(SG_GROUPS):
            wg = jnp.where(c_i <= r_i, sgw_ref[g], 0.0).astype(BF16)
            full = _dot(wg, vv[rows])
            mixed = mixed + jnp.where(lane_sg // SG_GROUP_DIM == g, full, 0.0)
        ysg_ref[rows, :] = (u[rows] * mixed).astype(BF16)

    @pl.when(i == 0)
    def _():
        ext_ref[0:CONV_HALO, :] = jnp.zeros((CONV_HALO, CONV_WIDTH), F32)

    ext_ref[CONV_HALO:, :] = p[:, P_CA:P_CG] * jax.nn.sigmoid(p[:, P_CG:P_COLS])
    acc = jnp.zeros((ROW_TILE, CONV_WIDTH), F32) + cb_ref[...]
    first = CONV_HALO - (CONV_TAPS - 1)
    for j in range(CONV_TAPS):
        acc = acc + cw_ref[j:j + 1, :] * ext_ref[first + j:first + j + ROW_TILE, :]
    ext_ref[0:CONV_HALO, :] = ext_ref[ROW_TILE:ROW_TILE + CONV_HALO, :]
    hc = _layer_norm(acc, clng_ref[...], clnb_ref[...])
    ycv_ref[...] = (hc * jax.nn.sigmoid(hc)).astype(BF16)


def _front(apply_ln, x, ln_g, ln_b, w_p, w_vt, cos, sin, sg_ln_g, sg_ln_b, sg_w, sg_bias, conv_w, conv_b,
           conv_ln_g, conv_ln_b):
    n_tiles = SEQ // ROW_TILE
    row = lambda w: pl.BlockSpec((ROW_TILE, w), lambda i: (i, 0))
    full = lambda shape: pl.BlockSpec(shape, lambda i: (0,) * len(shape))
    in_specs = [row(D_MODEL)]
    args = [x]
    if apply_ln:
        in_specs += [full((1, D_MODEL)), full((1, D_MODEL))]
        args += [ln_g, ln_b]
    in_specs += [full((D_MODEL, P_COLS)), full((ATT_WIDTH, D_MODEL)), row(LANES), row(LANES),
                 full((1, SG_WIDTH)), full((1, SG_WIDTH)),
                 full((SG_GROUPS, SG_CHUNK, SG_CHUNK)), full((SG_CHUNK, SG_WIDTH)),
                 full((CONV_TAPS, CONV_WIDTH)), full((1, CONV_WIDTH)), full((1, CONV_WIDTH)),
                 full((1, CONV_WIDTH))]
    args += [w_p, w_vt, cos, sin, sg_ln_g, sg_ln_b, sg_w, sg_bias, conv_w, conv_b, conv_ln_g, conv_ln_b]
    out_shape = [jax.ShapeDtypeStruct((SEQ, ATT_WIDTH), BF16),
                 jax.ShapeDtypeStruct((ATT_HEADS, SEQ, LANES), BF16),
                 jax.ShapeDtypeStruct((ATT_HEADS, VT_ROWS, SEQ), BF16),
                 jax.ShapeDtypeStruct((N_BLOCKS, SUBLANES, ATT_WIDTH), F32),
                 jax.ShapeDtypeStruct((SEQ, SG_WIDTH), BF16),
                 jax.ShapeDtypeStruct((SEQ, CONV_WIDTH), BF16)]
    out_specs = [row(ATT_WIDTH), pl.BlockSpec((ATT_HEADS, ROW_TILE, LANES), lambda i: (0, i, 0)),
                 pl.BlockSpec((ATT_HEADS, VT_ROWS, ROW_TILE), lambda i: (0, 0, i)),
                 pl.BlockSpec((1, SUBLANES, ATT_WIDTH), lambda i: (i, 0, 0)),
                 row(SG_WIDTH), row(CONV_WIDTH)]
    if apply_ln:
        out_shape = [jax.ShapeDtypeStruct((SEQ, D_MODEL), F32)] + out_shape
        out_specs = [row(D_MODEL)] + out_specs
    return pl.pallas_call(
        functools.partial(_front_kernel, apply_ln),
        grid=(n_tiles,),
        in_specs=in_specs,
        out_specs=out_specs,
        out_shape=out_shape,
        scratch_shapes=[pltpu.VMEM((CONV_HALO + ROW_TILE, CONV_WIDTH), F32)],
        compiler_params=pltpu.CompilerParams(dimension_semantics=("arbitrary",), vmem_limit_bytes=VMEM_LIMIT),
        name="front_ln" if apply_ln else "front",
    )(*args)


def _attn_kernel(q_ref, kx_ref, vt_ref, km_ref, o_ref, s0_ref, s1_ref, mx0_ref, mx1_ref):
    own = pl.program_id(1)
    q = q_ref[...]
    lane = lax.broadcasted_iota(jnp.int32, (ROW_TILE, LANES), 1)
    low = lane < HEAD_DIM
    blk = lane % HEAD_DIM

    km = km_ref[...]
    km_lane = lax.broadcasted_iota(jnp.int32, (N_BLOCKS, LANES), 1)
    km_rows = jnp.concatenate([jnp.where(km_lane >= HEAD_DIM, km, 0.0), jnp.where(km_lane < HEAD_DIM, km, 0.0)],
                              axis=0)
    km_hi, km_lo = _split_bf16(km_rows)
    gate = _dot_nt(q, km_hi) + _dot_nt(q, km_lo)
    gate = jnp.where(blk < own, gate, NEG)

    sel = jnp.zeros((ROW_TILE, LANES), jnp.bool_)
    lane_f = lane.astype(F32)
    own_v = jnp.broadcast_to(own, (ROW_TILE, LANES))
    for r in range(MOBA_TOPK):
        m_lo = jnp.max(jnp.where(low, gate, -jnp.inf), axis=1, keepdims=True)
        m_hi = jnp.max(jnp.where(low, -jnp.inf, gate), axis=1, keepdims=True)
        cand = jnp.where(gate == jnp.where(low, m_lo, m_hi), lane_f, float(LANES))
        i_lo = jnp.min(jnp.where(low, cand, float(LANES)), axis=1, keepdims=True)
        i_hi = jnp.min(jnp.where(low, float(LANES), cand), axis=1, keepdims=True)
        pick = lane_f == jnp.where(low, i_lo, i_hi)
        sel = sel | (pick & (own_v > r))
        gate = jnp.where(pick, -jnp.inf, gate)
    bias = jnp.where(sel, 0.0, NEG).astype(BF16)
    zero = jnp.zeros_like(q)
    qx_past = (jnp.where(low, q, bias), jnp.where(low, bias, q))
    qx_own = (jnp.where(low, q, zero), jnp.where(low, zero, q))

    key_i = lax.broadcasted_iota(jnp.int32, (MOBA_BLOCK, ROW_TILE), 0)
    qry_i = lax.broadcasted_iota(jnp.int32, (MOBA_BLOCK, ROW_TILE), 1)
    start_own = pl.multiple_of(own * MOBA_BLOCK, MOBA_BLOCK)

    state = []
    for par in range(2):
        s = _dot_nt(kx_ref[par, pl.ds(start_own, MOBA_BLOCK), :], qx_own[par])
        s = jnp.where(key_i <= qry_i, s, NEG)
        m = jnp.max(s, axis=0, keepdims=True)
        p = jnp.exp2(s - m)
        acc = _dot(vt_ref[par, :, pl.ds(start_own, MOBA_BLOCK)], p.astype(BF16))
        state += [m, acc]

    n_groups = lax.shift_right_logical(own + (ATT_GROUP - 1), ATT_GROUP_SHIFT)
    n_steps = lax.shift_right_logical(n_groups + 1, 1)
    last_group = N_BLOCKS // ATT_GROUP - 1

    def scores(g, s_ref, mx_ref):
        start = pl.multiple_of(g * ATT_GROUP_KEYS, ATT_GROUP_KEYS)
        for par in range(2):
            s = _dot_nt(kx_ref[par, pl.ds(start, ATT_GROUP_KEYS), :], qx_past[par])
            s_ref[par] = s
            mx_ref[par] = jnp.max(s, axis=0, keepdims=True)

    def consume(g, s_ref, mx_ref, carry):
        start = pl.multiple_of(g * ATT_GROUP_KEYS, ATT_GROUP_KEYS)
        out = []
        for par in range(2):
            m, acc = carry[2 * par:2 * par + 2]
            m_new = jnp.maximum(m, mx_ref[par])
            a = jnp.exp2(m - m_new)
            p = jnp.exp2(s_ref[par] - m_new)
            acc = a * acc + _dot(vt_ref[par, :, pl.ds(start, ATT_GROUP_KEYS)], p.astype(BF16))
            out += [m_new, acc]
        return tuple(out)

    @pl.when(n_steps > 0)
    def _():
        scores(0, s0_ref, mx0_ref)

    def body(i, carry):
        g = 2 * i
        scores(g + 1, s1_ref, mx1_ref)
        carry = consume(g, s0_ref, mx0_ref, carry)
        scores(jnp.minimum(g + 2, last_group), s0_ref, mx0_ref)
        return consume(g + 1, s1_ref, mx1_ref, carry)

    state = lax.fori_loop(0, n_steps, body, tuple(state))
    o_t = jnp.concatenate([acc[0:HEAD_DIM] / acc[HEAD_DIM:HEAD_DIM + 1] for acc in (state[1], state[3])],
                          axis=0)
    o_ref[...] = o_t.T.astype(BF16)


def _attn(q, kx, vt, kmean):
    n_pairs = ATT_WIDTH // LANES
    return pl.pallas_call(
        _attn_kernel,
        grid=(n_pairs, SEQ // ROW_TILE),
        in_specs=[pl.BlockSpec((ROW_TILE, LANES), lambda j, t: (t, j)),
                  pl.BlockSpec((2, SEQ, LANES), lambda j, t: (j, 0, 0)),
                  pl.BlockSpec((2, VT_ROWS, SEQ), lambda j, t: (j, 0, 0)),
                  pl.BlockSpec((N_BLOCKS, LANES), lambda j, t: (0, j))],
        out_specs=pl.BlockSpec((ROW_TILE, LANES), lambda j, t: (t, j)),
        out_shape=jax.ShapeDtypeStruct((SEQ, ATT_WIDTH), BF16),
        scratch_shapes=([pltpu.VMEM((2, ATT_GROUP_KEYS, ROW_TILE), F32)] * 2
                        + [pltpu.VMEM((2, 1, ROW_TILE), F32)] * 2),
        compiler_params=pltpu.CompilerParams(dimension_semantics=("arbitrary", "arbitrary"),
                                             vmem_limit_bytes=VMEM_LIMIT),
        name="attn",
    )(q, kx, vt, kmean)


def _outproj_kernel(h_ref, ya_ref, ys_ref, yc_ref, wo_ref, g_ref, b_ref, wr_ref, rb_ref, h1_ref, gates_ref):
    mix = (_dot(ya_ref[...], wo_ref[0:ATT_WIDTH, :])
           + _dot(ys_ref[...], wo_ref[ATT_WIDTH:ATT_WIDTH + SG_WIDTH, :])
           + _dot(yc_ref[...], wo_ref[ATT_WIDTH + SG_WIDTH:, :]))
    h1 = _layer_norm(DEEPNORM_ALPHA * h_ref[...] + mix, g_ref[...], b_ref[...])
    h1_ref[...] = h1

    w_hi, w_lo = _split_bf16(wr_ref[...])
    h_hi, h_lo = _split_bf16(h1)
    both = _dot_nt(jnp.concatenate([w_hi, w_lo], axis=0), h_hi)
    logits = both[0:N_EXPERTS] + both[N_EXPERTS:] + _dot_nt(w_hi, h_lo)
    scores = jax.nn.sigmoid(logits)
    biased = scores + rb_ref[...]
    sc = [scores[e:e + 1, :] for e in range(N_EXPERTS)]
    bi = [biased[e:e + 1, :] for e in range(N_EXPERTS)]

    best = None
    for g in range(N_EXPERT_GROUPS):
        a0, a1, a2, a3 = bi[4 * g:4 * g + 4]
        hi01, lo01 = jnp.maximum(a0, a1), jnp.minimum(a0, a1)
        hi23, lo23 = jnp.maximum(a2, a3), jnp.minimum(a2, a3)
        top1 = jnp.maximum(hi01, hi23)
        top2 = jnp.maximum(jnp.minimum(hi01, hi23), jnp.maximum(lo01, lo23))
        gs = top1 + top2
        if best is None:
            best, grp = gs, jnp.zeros_like(gs, dtype=jnp.int32)
        else:
            upd = gs > best
            best = jnp.where(upd, gs, best)
            grp = jnp.where(upd, g, grp)

    def in_group(vals, k):
        out = vals[k]
        for g in range(1, N_EXPERT_GROUPS):
            out = jnp.where(grp == g, vals[4 * g + k], out)
        return out

    vb = [in_group(bi, k) for k in range(EXPERTS_PER_GROUP)]
    vs = [in_group(sc, k) for k in range(EXPERTS_PER_GROUP)]
    m1, i1, s1 = vb[0], jnp.zeros_like(grp), vs[0]
    for k in range(1, EXPERTS_PER_GROUP):
        upd = vb[k] > m1
        m1 = jnp.where(upd, vb[k], m1)
        i1 = jnp.where(upd, k, i1)
        s1 = jnp.where(upd, vs[k], s1)
    m2 = jnp.full_like(m1, -jnp.inf)
    i2 = jnp.zeros_like(grp)
    s2 = jnp.zeros_like(s1)
    for k in range(EXPERTS_PER_GROUP):
        upd = (i1 != k) & (vb[k] > m2)
        m2 = jnp.where(upd, vb[k], m2)
        i2 = jnp.where(upd, k, i2)
        s2 = jnp.where(upd, vs[k], s2)
    tot = s1 + s2
    e1 = grp * EXPERTS_PER_GROUP + i1
    e2 = grp * EXPERTS_PER_GROUP + i2
    e_idx = lax.broadcasted_iota(jnp.int32, (N_EXPERTS, ROW_TILE), 0)
    gates_ref[...] = jnp.where(e_idx == e1, s1 / tot, 0.0) + jnp.where(e_idx == e2, s2 / tot, 0.0)


def _outproj(h, y_att, y_sg, y_cv, w_out, ln_g, ln_b, w_router_t, router_bias):
    row = lambda w: pl.BlockSpec((ROW_TILE, w), lambda i: (i, 0))
    full = lambda shape: pl.BlockSpec(shape, lambda i: (0,) * len(shape))
    return pl.pallas_call(
        _outproj_kernel,
        grid=(SEQ // ROW_TILE,),
        in_specs=[row(D_MODEL), row(ATT_WIDTH), row(SG_WIDTH), row(CONV_WIDTH), full((D_MODEL, D_MODEL)),
                  full((1, D_MODEL)), full((1, D_MODEL)), full((N_EXPERTS, D_MODEL)), full((N_EXPERTS, 1))],
        out_specs=[row(D_MODEL), pl.BlockSpec((N_EXPERTS, ROW_TILE), lambda i: (0, i))],
        out_shape=[jax.ShapeDtypeStruct((SEQ, D_MODEL), F32), jax.ShapeDtypeStruct((N_EXPERTS, SEQ), F32)],
        compiler_params=pltpu.CompilerParams(dimension_semantics=("arbitrary",), vmem_limit_bytes=VMEM_LIMIT),
        name="outproj",
    )(h, y_att, y_sg, y_cv, w_out, ln_g, ln_b, w_router_t, router_bias)


def _moe_kernel(h_ref, gates_ref, wg_ref, wu_ref, wd_ref, g_ref, b_ref, o_ref, acc_ref, xb_ref):
    e = pl.program_id(1)

    @pl.when(e == 0)
    def _():
        acc_ref[...] = jnp.zeros_like(acc_ref)
        xb_ref[...] = h_ref[...].astype(BF16)

    x = xb_ref[...]
    a = jax.nn.silu(_dot(x, wg_ref[0])) * _dot(x, wu_ref[0])
    y = _dot(a.astype(BF16), wd_ref[0])
    lane = lax.broadcasted_iota(jnp.int32, (MOE_ROW_TILE, N_EXPERTS), 1)
    gate = jnp.sum(jnp.where(lane == e, gates_ref[...], 0.0), axis=1, keepdims=True)
    acc_ref[...] += gate * y

    @pl.when(e == N_EXPERTS - 1)
    def _():
        o_ref[...] = _layer_norm(DEEPNORM_ALPHA * h_ref[...] + acc_ref[...], g_ref[...], b_ref[...])


def _moe(h, gates, w_gate, w_up, w_down, ln_g, ln_b):
    return pl.pallas_call(
        _moe_kernel,
        grid=(SEQ // MOE_ROW_TILE, N_EXPERTS),
        in_specs=[pl.BlockSpec((MOE_ROW_TILE, D_MODEL), lambda i, e: (i, 0)),
                  pl.BlockSpec((MOE_ROW_TILE, N_EXPERTS), lambda i, e: (i, 0)),
                  pl.BlockSpec((1, D_MODEL, D_EXPERT), lambda i, e: (e, 0, 0)),
                  pl.BlockSpec((1, D_MODEL, D_EXPERT), lambda i, e: (e, 0, 0)),
                  pl.BlockSpec((1, D_EXPERT, D_MODEL), lambda i, e: (e, 0, 0)),
                  pl.BlockSpec((1, D_MODEL), lambda i, e: (0, 0)),
                  pl.BlockSpec((1, D_MODEL), lambda i, e: (0, 0))],
        out_specs=pl.BlockSpec((MOE_ROW_TILE, D_MODEL), lambda i, e: (i, 0)),
        out_shape=jax.ShapeDtypeStruct((SEQ, D_MODEL), F32),
        scratch_shapes=[pltpu.VMEM((MOE_ROW_TILE, D_MODEL), F32), pltpu.VMEM((MOE_ROW_TILE, D_MODEL), BF16)],
        compiler_params=pltpu.CompilerParams(dimension_semantics=("arbitrary", "arbitrary"),
                                             vmem_limit_bytes=VMEM_LIMIT),
        name="moe",
    )(h, gates, w_gate, w_up, w_down, ln_g, ln_b)


def _rope_tables():
    half = HEAD_DIM // 2
    inv_freq = ROPE_THETA ** (-jnp.arange(half, dtype=F32) / half)
    ang = jnp.arange(SEQ, dtype=F32)[:, None] * inv_freq[None, :]
    cos, sin = jnp.cos(ang), jnp.sin(ang)
    reps = LANES // HEAD_DIM
    return jnp.tile(jnp.concatenate([cos, cos], axis=1), (1, reps)), jnp.tile(jnp.concatenate([-sin, sin], axis=1), (1, reps))


def kernel(x, ln_in_g, ln_in_b, w_in, w_out, sg_ln_g, sg_ln_b, sg_w, sg_b, conv_w, conv_b, conv_ln_g, conv_ln_b, ln_mix_g, ln_mix_b, w_router, router_bias, w_gate, w_up, w_down, ln_ffn_g, ln_ffn_b):
    assert x.shape == (1, SEQ, D_MODEL) and w_in.shape == (DEPTH, D_MODEL, IN_COLS)
    cos, sin = _rope_tables()
    row = lambda a: a.reshape(1, -1)
    w_router_t = w_router.T
    rb = router_bias.reshape(N_EXPERTS, 1)
    h = x.reshape(SEQ, D_MODEL)
    for l in range(DEPTH):
        sg_bias = jnp.repeat(sg_b[l].T, SG_GROUP_DIM, axis=1)
        w_p = jnp.concatenate([w_in[l][:, :V_OFF], w_in[l][:, SGU_OFF:]], axis=1).astype(BF16)
        w_vt = w_in[l][:, V_OFF:SGU_OFF].T.astype(BF16)
        args = (w_p, w_vt, cos, sin, row(sg_ln_g[l]), row(sg_ln_b[l]), sg_w[l], sg_bias,
                conv_w[l], row(conv_b[l]), row(conv_ln_g[l]), row(conv_ln_b[l]))
        if l == 0:
            h, q, kx, vt, kmean, y_sg, y_cv = _front(True, h, row(ln_in_g), row(ln_in_b), *args)
        else:
            q, kx, vt, kmean, y_sg, y_cv = _front(False, h, None, None, *args)
        y_att = _attn(q, kx, vt, kmean[:, 0, :])
        h, gates_t = _outproj(h, y_att, y_sg, y_cv, w_out[l].astype(BF16), row(ln_mix_g[l]), row(ln_mix_b[l]),
                              w_router_t, rb)
        h = _moe(h, gates_t.T, w_gate[l].astype(BF16), w_up[l].astype(BF16), w_down[l].astype(BF16),
                 row(ln_ffn_g[l]), row(ln_ffn_b[l]))
    return h.reshape(1, SEQ, D_MODEL)
```

```python
import functools
import math

import jax
import jax.numpy as jnp
from jax import lax
from jax.experimental import pallas as pl
from jax.experimental.pallas import tpu as pltpu

D_MODEL = 1024
SEQ = 16384
DEPTH = 2
HEAD_DIM = 64
ATT_WIDTH = 512
ATT_HEADS = 8
SG_WIDTH = 256
SG_GROUPS = 4
SG_GROUP_DIM = 64
CONV_WIDTH = 256
K_OFF = 512
V_OFF = 1024
SGU_OFF = 1536
SGV_OFF = 1792
CA_OFF = 2048
CG_OFF = 2304
IN_COLS = 2560
P_SGU = SGU_OFF - ATT_WIDTH
P_SGV = SGV_OFF - ATT_WIDTH
P_CA = CA_OFF - ATT_WIDTH
P_CG = CG_OFF - ATT_WIDTH
P_COLS = IN_COLS - ATT_WIDTH
MOBA_BLOCK = 256
MOBA_TOPK = 3
N_BLOCKS = SEQ // MOBA_BLOCK
SG_CHUNK = 128
CONV_TAPS = 31
ROPE_THETA = 10000.0
N_EXPERTS = 16
N_EXPERT_GROUPS = 4
EXPERTS_PER_GROUP = 4
D_EXPERT = 512
LN_EPS = 1e-5
NEG = -1e30
DEEPNORM_ALPHA = (2.0 * DEPTH) ** 0.25

LANES = 128
SUBLANES = 8
ROW_TILE = MOBA_BLOCK
CONV_HALO = 32
LOG2_E = math.log2(math.e)
VT_ROWS = HEAD_DIM + 16
ATT_GROUP_SHIFT = 2
ATT_GROUP = 1 << ATT_GROUP_SHIFT
ATT_GROUP_KEYS = ATT_GROUP * MOBA_BLOCK
MOE_TILE = 256
MOE_SRC_TILE = 512
MOE_MAX_TILES = SEQ // MOE_TILE + N_EXPERT_GROUPS
MOE_MAX_PAIRS = MOE_MAX_TILES + N_EXPERT_GROUPS * (SEQ // MOE_SRC_TILE)
MOE_SORTED_ROWS = MOE_MAX_TILES * MOE_TILE
VMEM_LIMIT = 48 * 1024 * 1024

F32 = jnp.float32
BF16 = jnp.bfloat16


def _dot(a, b):
    return jnp.dot(a, b, preferred_element_type=F32)


def _dot_nt(a, b):
    return lax.dot_general(a, b, (((1,), (1,)), ((), ())), preferred_element_type=F32)


def _layer_norm(x, g, b):
    mu = jnp.mean(x, axis=-1, keepdims=True)
    xc = x - mu
    var = jnp.mean(xc * xc, axis=-1, keepdims=True)
    return xc * lax.rsqrt(var + LN_EPS) * g + b


def _split_bf16(x):
    hi = x.astype(BF16)
    lo = (x - hi.astype(F32)).astype(BF16)
    return hi, lo


def _front_kernel(apply_ln, *refs):
    if apply_ln:
        (x_ref, lng_ref, lnb_ref, w_ref, wvt_ref, cos_ref, sin_ref, sglng_ref, sglnb_ref, sgw_ref, sgb_ref,
         cw_ref, cb_ref, clng_ref, clnb_ref,
         h_ref, q_ref, kx_ref, vt_ref, km_ref, ysg_ref, ycv_ref, ext_ref) = refs
    else:
        (x_ref, w_ref, wvt_ref, cos_ref, sin_ref, sglng_ref, sglnb_ref, sgw_ref, sgb_ref,
         cw_ref, cb_ref, clng_ref, clnb_ref,
         q_ref, kx_ref, vt_ref, km_ref, ysg_ref, ycv_ref, ext_ref) = refs
    i = pl.program_id(0)
    h = x_ref[...]
    if apply_ln:
        h = _layer_norm(h, lng_ref[...], lnb_ref[...])
        h_ref[...] = h
    hb = h.astype(BF16)
    p = _dot(hb, w_ref[...])
    vt_ref[:, 0:HEAD_DIM, :] = _dot_nt(wvt_ref[...], hb).reshape(ATT_HEADS, HEAD_DIM, ROW_TILE).astype(BF16)
    vt_ref[:, HEAD_DIM:, :] = jnp.ones((ATT_HEADS, VT_ROWS - HEAD_DIM, ROW_TILE), BF16)

    lane = lax.broadcasted_iota(jnp.int32, (ROW_TILE, LANES), 1)
    first_half = (lane % HEAD_DIM) < (HEAD_DIM // 2)
    low_head = lane < HEAD_DIM
    cos = cos_ref[...]
    sin = sin_ref[...]

    def rot(x):
        swapped = jnp.where(first_half, pltpu.roll(x, LANES - HEAD_DIM // 2, 1), pltpu.roll(x, HEAD_DIM // 2, 1))
        return x * cos + swapped * sin

    scale = HEAD_DIM ** -0.5 * LOG2_E
    for j in range(ATT_WIDTH // LANES):
        sl = slice(j * LANES, (j + 1) * LANES)
        q_ref[:, sl] = (rot(p[:, sl]) * scale).astype(BF16)
        kr = rot(p[:, K_OFF + j * LANES:K_OFF + (j + 1) * LANES])
        km_ref[0, :, sl] = jnp.broadcast_to(jnp.mean(kr, axis=0, keepdims=True), (SUBLANES, LANES))
        kx_ref[2 * j] = jnp.where(low_head, kr, (lane - HEAD_DIM == i).astype(F32)).astype(BF16)
        kx_ref[2 * j + 1] = jnp.where(low_head, (lane == i).astype(F32), kr).astype(BF16)

    u = jax.nn.gelu(p[:, P_SGU:P_SGV])
    vv = _layer_norm(jax.nn.gelu(p[:, P_SGV:P_CA]), sglng_ref[...], sglnb_ref[...]).astype(BF16)
    r_i = lax.broadcasted_iota(jnp.int32, (SG_CHUNK, SG_CHUNK), 0)
    c_i = lax.broadcasted_iota(jnp.int32, (SG_CHUNK, SG_CHUNK), 1)
    lane_sg = lax.broadcasted_iota(jnp.int32, (SG_CHUNK, SG_WIDTH), 1)
    for c in range(ROW_TILE // SG_CHUNK):
        rows = slice(c * SG_CHUNK, (c + 1) * SG_CHUNK)
        mixed = sgb_ref[...]
        for g in range(SG_GROUPS):
            wg = jnp.where(c_i <= r_i, sgw_ref[g], 0.0).astype(BF16)
            full = _dot(wg, vv[rows])
            mixed = mixed + jnp.where(lane_sg // SG_GROUP_DIM == g, full, 0.0)
        ysg_ref[rows, :] = (u[rows] * mixed).astype(BF16)

    @pl.when(i == 0)
    def _():
        ext_ref[0:CONV_HALO, :] = jnp.zeros((CONV_HALO, CONV_WIDTH), F32)

    ext_ref[CONV_HALO:, :] = p[:, P_CA:P_CG] * jax.nn.sigmoid(p[:, P_CG:P_COLS])
    acc = jnp.zeros((ROW_TILE, CONV_WIDTH), F32) + cb_ref[...]
    first = CONV_HALO - (CONV_TAPS - 1)
    for j in range(CONV_TAPS):
        acc = acc + cw_ref[j:j + 1, :] * ext_ref[first + j:first + j + ROW_TILE, :]
    ext_ref[0:CONV_HALO, :] = ext_ref[ROW_TILE:ROW_TILE + CONV_HALO, :]
    hc = _layer_norm(acc, clng_ref[...], clnb_ref[...])
    ycv_ref[...] = (hc * jax.nn.sigmoid(hc)).astype(BF16)


def _front(apply_ln, x, ln_g, ln_b, w_p, w_vt, cos, sin, sg_ln_g, sg_ln_b, sg_w, sg_bias, conv_w, conv_b,
           conv_ln_g, conv_ln_b):
    n_tiles = SEQ // ROW_TILE
    row = lambda w: pl.BlockSpec((ROW_TILE, w), lambda i: (i, 0))
    full = lambda shape: pl.BlockSpec(shape, lambda i: (0,) * len(shape))
    in_specs = [row(D_MODEL)]
    args = [x]
    if apply_ln:
        in_specs += [full((1, D_MODEL)), full((1, D_MODEL))]
        args += [ln_g, ln_b]
    in_specs += [full((D_MODEL, P_COLS)), full((ATT_WIDTH, D_MODEL)), row(LANES), row(LANES),
                 full((1, SG_WIDTH)), full((1, SG_WIDTH)),
                 full((SG_GROUPS, SG_CHUNK, SG_CHUNK)), full((SG_CHUNK, SG_WIDTH)),
                 full((CONV_TAPS, CONV_WIDTH)), full((1, CONV_WIDTH)), full((1, CONV_WIDTH)),
                 full((1, CONV_WIDTH))]
    args += [w_p, w_vt, cos, sin, sg_ln_g, sg_ln_b, sg_w, sg_bias, conv_w, conv_b, conv_ln_g, conv_ln_b]
    out_shape = [jax.ShapeDtypeStruct((SEQ, ATT_WIDTH), BF16),
                 jax.ShapeDtypeStruct((ATT_HEADS, SEQ, LANES), BF16),
                 jax.ShapeDtypeStruct((ATT_HEADS, VT_ROWS, SEQ), BF16),
                 jax.ShapeDtypeStruct((N_BLOCKS, SUBLANES, ATT_WIDTH), F32),
                 jax.ShapeDtypeStruct((SEQ, SG_WIDTH), BF16),
                 jax.ShapeDtypeStruct((SEQ, CONV_WIDTH), BF16)]
    out_specs = [row(ATT_WIDTH), pl.BlockSpec((ATT_HEADS, ROW_TILE, LANES), lambda i: (0, i, 0)),
                 pl.BlockSpec((ATT_HEADS, VT_ROWS, ROW_TILE), lambda i: (0, 0, i)),
                 pl.BlockSpec((1, SUBLANES, ATT_WIDTH), lambda i: (i, 0, 0)),
                 row(SG_WIDTH), row(CONV_WIDTH)]
    if apply_ln:
        out_shape = [jax.ShapeDtypeStruct((SEQ, D_MODEL), F32)] + out_shape
        out_specs = [row(D_MODEL)] + out_specs
    return pl.pallas_call(
        functools.partial(_front_kernel, apply_ln),
        grid=(n_tiles,),
        in_specs=in_specs,
        out_specs=out_specs,
        out_shape=out_shape,
        scratch_shapes=[pltpu.VMEM((CONV_HALO + ROW_TILE, CONV_WIDTH), F32)],
        compiler_params=pltpu.CompilerParams(dimension_semantics=("arbitrary",), vmem_limit_bytes=VMEM_LIMIT),
        name="front_ln" if apply_ln else "front",
    )(*args)


def _attn_kernel(q_ref, kx_ref, vt_ref, km_ref, o_ref, s0_ref, s1_ref, mx0_ref, mx1_ref):
    own = pl.program_id(1)
    q = q_ref[...]
    lane = lax.broadcasted_iota(jnp.int32, (ROW_TILE, LANES), 1)
    low = lane < HEAD_DIM
    blk = lane % HEAD_DIM

    km = km_ref[...]
    km_lane = lax.broadcasted_iota(jnp.int32, (N_BLOCKS, LANES), 1)
    km_rows = jnp.concatenate([jnp.where(km_lane >= HEAD_DIM, km, 0.0), jnp.where(km_lane < HEAD_DIM, km, 0.0)],
                              axis=0)
    km_hi, km_lo = _split_bf16(km_rows)
    gate = _dot_nt(q, km_hi) + _dot_nt(q, km_lo)
    gate = jnp.where(blk < own, gate, NEG)

    sel = jnp.zeros((ROW_TILE, LANES), jnp.bool_)
    lane_f = lane.astype(F32)
    own_v = jnp.broadcast_to(own, (ROW_TILE, LANES))
    for r in range(MOBA_TOPK):
        m_lo = jnp.max(jnp.where(low, gate, -jnp.inf), axis=1, keepdims=True)
        m_hi = jnp.max(jnp.where(low, -jnp.inf, gate), axis=1, keepdims=True)
        cand = jnp.where(gate == jnp.where(low, m_lo, m_hi), lane_f, float(LANES))
        i_lo = jnp.min(jnp.where(low, cand, float(LANES)), axis=1, keepdims=True)
        i_hi = jnp.min(jnp.where(low, float(LANES), cand), axis=1, keepdims=True)
        pick = lane_f == jnp.where(low, i_lo, i_hi)
        sel = sel | (pick & (own_v > r))
        gate = jnp.where(pick, -jnp.inf, gate)
    bias = jnp.where(sel, 0.0, NEG).astype(BF16)
    zero = jnp.zeros_like(q)
    qx_past = (jnp.where(low, q, bias), jnp.where(low, bias, q))
    qx_own = (jnp.where(low, q, zero), jnp.where(low, zero, q))

    key_i = lax.broadcasted_iota(jnp.int32, (MOBA_BLOCK, ROW_TILE), 0)
    qry_i = lax.broadcasted_iota(jnp.int32, (MOBA_BLOCK, ROW_TILE), 1)
    start_own = pl.multiple_of(own * MOBA_BLOCK, MOBA_BLOCK)

    state = []
    for par in range(2):
        s = _dot_nt(kx_ref[par, pl.ds(start_own, MOBA_BLOCK), :], qx_own[par])
        s = jnp.where(key_i <= qry_i, s, NEG)
        m = jnp.max(s, axis=0, keepdims=True)
        p = jnp.exp2(s - m)
        acc = _dot(vt_ref[par, :, pl.ds(start_own, MOBA_BLOCK)], p.astype(BF16))
        state += [m, acc]

    n_groups = lax.shift_right_logical(own + (ATT_GROUP - 1), ATT_GROUP_SHIFT)
    last_group = N_BLOCKS // ATT_GROUP - 1

    def scores(g, s_ref, mx_ref):
        start = pl.multiple_of(g * ATT_GROUP_KEYS, ATT_GROUP_KEYS)
        for par in range(2):
            s = _dot_nt(kx_ref[par, pl.ds(start, ATT_GROUP_KEYS), :], qx_past[par])
            s_ref[par] = s
            mx_ref[par] = jnp.max(s, axis=0, keepdims=True)

    def consume(g, s_ref, mx_ref, carry):
        start = pl.multiple_of(g * ATT_GROUP_KEYS, ATT_GROUP_KEYS)
        out = []
        for par in range(2):
            m, acc = carry[2 * par:2 * par + 2]
            m_new = jnp.maximum(m, mx_ref[par])
            a = jnp.exp2(m - m_new)
            p = jnp.exp2(s_ref[par] - m_new)
            acc = a * acc + _dot(vt_ref[par, :, pl.ds(start, ATT_GROUP_KEYS)], p.astype(BF16))
            out += [m_new, acc]
        return tuple(out)

    @pl.when(n_groups > 0)
    def _():
        scores(0, s0_ref, mx0_ref)

    def body(g, carry):
        g_next = jnp.minimum(g + 1, last_group)

        def even(c):
            scores(g_next, s1_ref, mx1_ref)
            return consume(g, s0_ref, mx0_ref, c)

        def odd(c):
            scores(g_next, s0_ref, mx0_ref)
            return consume(g, s1_ref, mx1_ref, c)

        return lax.cond((g & 1) == 0, even, odd, carry)

    state = lax.fori_loop(0, n_groups, body, tuple(state))
    o_t = jnp.concatenate([acc[0:HEAD_DIM] / acc[HEAD_DIM:HEAD_DIM + 1] for acc in (state[1], state[3])],
                          axis=0)
    o_ref[...] = o_t.T.astype(BF16)


def _attn(q, kx, vt, kmean):
    n_pairs = ATT_WIDTH // LANES
    return pl.pallas_call(
        _attn_kernel,
        grid=(n_pairs, SEQ // ROW_TILE),
        in_specs=[pl.BlockSpec((ROW_TILE, LANES), lambda j, t: (t, j)),
                  pl.BlockSpec((2, SEQ, LANES), lambda j, t: (j, 0, 0)),
                  pl.BlockSpec((2, VT_ROWS, SEQ), lambda j, t: (j, 0, 0)),
                  pl.BlockSpec((N_BLOCKS, LANES), lambda j, t: (0, j))],
        out_specs=pl.BlockSpec((ROW_TILE, LANES), lambda j, t: (t, j)),
        out_shape=jax.ShapeDtypeStruct((SEQ, ATT_WIDTH), BF16),
        scratch_shapes=([pltpu.VMEM((2, ATT_GROUP_KEYS, ROW_TILE), F32)] * 2
                        + [pltpu.VMEM((2, 1, ROW_TILE), F32)] * 2),
        compiler_params=pltpu.CompilerParams(dimension_semantics=("arbitrary", "arbitrary"),
                                             vmem_limit_bytes=VMEM_LIMIT),
        name="attn",
    )(q, kx, vt, kmean)


def _outproj_kernel(h_ref, ya_ref, ys_ref, yc_ref, wo_ref, g_ref, b_ref, wr_ref, rb_ref,
                    h1_ref, h1b_ref, grp_ref, rank_ref, wk_ref, cnt_ref):
    mix = (_dot(ya_ref[...], wo_ref[0:ATT_WIDTH, :])
           + _dot(ys_ref[...], wo_ref[ATT_WIDTH:ATT_WIDTH + SG_WIDTH, :])
           + _dot(yc_ref[...], wo_ref[ATT_WIDTH + SG_WIDTH:, :]))
    h1 = _layer_norm(DEEPNORM_ALPHA * h_ref[...] + mix, g_ref[...], b_ref[...])
    h1_ref[...] = h1
    h1b_ref[...] = h1.astype(BF16)

    w_hi, w_lo = _split_bf16(wr_ref[...])
    h_hi, h_lo = _split_bf16(h1)
    both = _dot_nt(jnp.concatenate([w_hi, w_lo], axis=0), h_hi)
    logits = both[0:N_EXPERTS] + both[N_EXPERTS:] + _dot_nt(w_hi, h_lo)
    scores = jax.nn.sigmoid(logits)
    biased = scores + rb_ref[...]
    sc = [scores[e:e + 1, :] for e in range(N_EXPERTS)]
    bi = [biased[e:e + 1, :] for e in range(N_EXPERTS)]

    best = None
    for g in range(N_EXPERT_GROUPS):
        a0, a1, a2, a3 = bi[4 * g:4 * g + 4]
        hi01, lo01 = jnp.maximum(a0, a1), jnp.minimum(a0, a1)
        hi23, lo23 = jnp.maximum(a2, a3), jnp.minimum(a2, a3)
        top1 = jnp.maximum(hi01, hi23)
        top2 = jnp.maximum(jnp.minimum(hi01, hi23), jnp.maximum(lo01, lo23))
        gs = top1 + top2
        if best is None:
            best, grp = gs, jnp.zeros_like(gs, dtype=jnp.int32)
        else:
            upd = gs > best
            best = jnp.where(upd, gs, best)
            grp = jnp.where(upd, g, grp)

    def in_group(vals, k):
        out = vals[k]
        for g in range(1, N_EXPERT_GROUPS):
            out = jnp.where(grp == g, vals[4 * g + k], out)
        return out

    vb = [in_group(bi, k) for k in range(EXPERTS_PER_GROUP)]
    vs = [in_group(sc, k) for k in range(EXPERTS_PER_GROUP)]
    m1, i1, s1 = vb[0], jnp.zeros_like(grp), vs[0]
    for k in range(1, EXPERTS_PER_GROUP):
        upd = vb[k] > m1
        m1 = jnp.where(upd, vb[k], m1)
        i1 = jnp.where(upd, k, i1)
        s1 = jnp.where(upd, vs[k], s1)
    m2 = jnp.full_like(m1, -jnp.inf)
    i2 = jnp.zeros_like(grp)
    s2 = jnp.zeros_like(s1)
    for k in range(EXPERTS_PER_GROUP):
        upd = (i1 != k) & (vb[k] > m2)
        m2 = jnp.where(upd, vb[k], m2)
        i2 = jnp.where(upd, k, i2)
        s2 = jnp.where(upd, vs[k], s2)
    tot = s1 + s2
    k_idx = lax.broadcasted_iota(jnp.int32, (EXPERTS_PER_GROUP, ROW_TILE), 0)
    wk_ref[...] = jnp.where(k_idx == i1, s1 / tot, 0.0) + jnp.where(k_idx == i2, s2 / tot, 0.0)
    grp_ref[...] = grp

    @pl.when(pl.program_id(0) == 0)
    def _():
        cnt_ref[...] = jnp.zeros_like(cnt_ref)

    member = (k_idx == grp).astype(F32)
    t_r = lax.broadcasted_iota(jnp.int32, (ROW_TILE, ROW_TILE), 0)
    t_c = lax.broadcasted_iota(jnp.int32, (ROW_TILE, ROW_TILE), 1)
    earlier = _dot(member.astype(BF16), (t_r < t_c).astype(BF16)) + cnt_ref[...]
    rank_ref[...] = jnp.sum(member * earlier, axis=0, keepdims=True).astype(jnp.int32)
    cnt_ref[...] += jnp.sum(member, axis=1, keepdims=True)


def _outproj(h, y_att, y_sg, y_cv, w_out, ln_g, ln_b, w_router_t, router_bias):
    row = lambda w: pl.BlockSpec((ROW_TILE, w), lambda i: (i, 0))
    full = lambda shape: pl.BlockSpec(shape, lambda i: (0,) * len(shape))
    lane_row = lambda n: pl.BlockSpec((n, ROW_TILE), lambda i: (0, i))
    return pl.pallas_call(
        _outproj_kernel,
        grid=(SEQ // ROW_TILE,),
        in_specs=[row(D_MODEL), row(ATT_WIDTH), row(SG_WIDTH), row(CONV_WIDTH), full((D_MODEL, D_MODEL)),
                  full((1, D_MODEL)), full((1, D_MODEL)), full((N_EXPERTS, D_MODEL)), full((N_EXPERTS, 1))],
        out_specs=[row(D_MODEL), row(D_MODEL), lane_row(1), lane_row(1), lane_row(EXPERTS_PER_GROUP)],
        out_shape=[jax.ShapeDtypeStruct((SEQ, D_MODEL), F32),
                   jax.ShapeDtypeStruct((SEQ, D_MODEL), BF16),
                   jax.ShapeDtypeStruct((1, SEQ), jnp.int32),
                   jax.ShapeDtypeStruct((1, SEQ), jnp.int32),
                   jax.ShapeDtypeStruct((EXPERTS_PER_GROUP, SEQ), F32)],
        scratch_shapes=[pltpu.VMEM((N_EXPERT_GROUPS, ROW_TILE), F32)],
        compiler_params=pltpu.CompilerParams(dimension_semantics=("arbitrary",), vmem_limit_bytes=VMEM_LIMIT),
        name="outproj",
    )(h, y_att, y_sg, y_cv, w_out, ln_g, ln_b, w_router_t, router_bias)


PAIR_VALID, PAIR_FIRST, PAIR_LAST = 1, 2, 4


def _moe_plan(grp, rank):
    n_src = SEQ // MOE_SRC_TILE
    member = (grp[:, None] == jnp.arange(N_EXPERT_GROUPS)[None, :]).astype(jnp.int32)
    cnt = member.reshape(n_src, MOE_SRC_TILE, N_EXPERT_GROUPS).sum(axis=1).T
    start = jnp.cumsum(cnt, axis=1) - cnt
    tiles = (cnt.sum(axis=1) + MOE_TILE - 1) // MOE_TILE
    tile0 = jnp.cumsum(tiles) - tiles
    pos = tile0[grp] * MOE_TILE + rank

    lo = jnp.arange(SEQ // MOE_TILE)[None, :, None] * MOE_TILE
    overlap = (cnt[:, None, :] > 0) & (start[:, None, :] < lo + MOE_TILE) & ((start + cnt)[:, None, :] > lo)
    n_pairs = overlap.sum()
    idx = jnp.arange(MOE_MAX_PAIRS)
    keep = jnp.minimum(idx, n_pairs - 1)

    def pair_list(mask, order):
        ix = jnp.nonzero(mask, size=MOE_MAX_PAIRS, fill_value=0)
        g, il, r = (ix[order.index(a)][keep] for a in "gtr")
        tile = tile0[g] + il
        key = tile if order[0] == "g" else r
        valid = idx < n_pairs
        first = valid & ((idx == 0) | (key != jnp.roll(key, 1)))
        last = valid & ((idx == n_pairs - 1) | (key != jnp.roll(key, -1)))
        flags = PAIR_VALID * valid + PAIR_FIRST * first + PAIR_LAST * last
        return tile.astype(jnp.int32), r.astype(jnp.int32), g.astype(jnp.int32), flags.astype(jnp.int32)

    return pos, pair_list(overlap, "gtr"), pair_list(overlap.transpose(2, 0, 1), "rgt")


def _moe_ffn_kernel(pt_ref, pr_ref, pg_ref, pf_ref, h_ref, pos_ref, wk_ref, wg_ref, wu_ref, wd_ref, y_ref,
                    x_ref, w_ref):
    p = pl.program_id(0)
    flags = pf_ref[p]

    @pl.when((flags & PAIR_FIRST) != 0)
    def _():
        x_ref[...] = jnp.zeros_like(x_ref)
        w_ref[...] = jnp.zeros_like(w_ref)

    @pl.when((flags & PAIR_VALID) != 0)
    def _():
        rows = pt_ref[p] * MOE_TILE + lax.broadcasted_iota(jnp.int32, (MOE_TILE, MOE_SRC_TILE), 0)
        onehot = jnp.where(pos_ref[...] == rows, 1.0, 0.0).astype(BF16)
        x_ref[...] += _dot(onehot, h_ref[...])
        wk = wk_ref[...]
        for _ in range(3):
            piece = wk.astype(BF16)
            w_ref[...] += _dot_nt(onehot, piece)
            wk = wk - piece.astype(F32)

    @pl.when((flags & PAIR_LAST) != 0)
    def _():
        x = x_ref[...].astype(BF16)
        w = w_ref[...]
        y = jnp.zeros((MOE_TILE, D_MODEL), F32)
        for k in range(EXPERTS_PER_GROUP):
            a = jax.nn.silu(_dot(x, wg_ref[k])) * _dot(x, wu_ref[k])
            y = y + w[:, k:k + 1] * _dot(a.astype(BF16), wd_ref[k])
        y_ref[...] = y


def _moe_ffn(plan, h_bf16, pos, wk, w_gate, w_up, w_down):
    tile, src, grp, flags = plan
    src_rows = lambda w: pl.BlockSpec((MOE_SRC_TILE, w), lambda p, pt, pr, pg, pf: (pr[p], 0))
    src_lanes = lambda n: pl.BlockSpec((n, MOE_SRC_TILE), lambda p, pt, pr, pg, pf: (0, pr[p]))
    experts = lambda a, b: pl.BlockSpec((EXPERTS_PER_GROUP, a, b), lambda p, pt, pr, pg, pf: (pg[p], 0, 0))
    return pl.pallas_call(
        _moe_ffn_kernel,
        grid_spec=pltpu.PrefetchScalarGridSpec(
            num_scalar_prefetch=4,
            grid=(MOE_MAX_PAIRS,),
            in_specs=[src_rows(D_MODEL), src_lanes(1), src_lanes(EXPERTS_PER_GROUP),
                      experts(D_MODEL, D_EXPERT), experts(D_MODEL, D_EXPERT), experts(D_EXPERT, D_MODEL)],
            out_specs=pl.BlockSpec((MOE_TILE, D_MODEL), lambda p, pt, pr, pg, pf: (pt[p], 0)),
            scratch_shapes=[pltpu.VMEM((MOE_TILE, D_MODEL), F32), pltpu.VMEM((MOE_TILE, EXPERTS_PER_GROUP), F32)]),
        out_shape=jax.ShapeDtypeStruct((MOE_SORTED_ROWS, D_MODEL), F32),
        compiler_params=pltpu.CompilerParams(dimension_semantics=("arbitrary",), vmem_limit_bytes=VMEM_LIMIT),
        name="moe_ffn",
    )(tile, src, grp, flags, h_bf16, pos, wk, w_gate, w_up, w_down)


def _moe_combine_kernel(pt_ref, pr_ref, pg_ref, pf_ref, y_ref, pos_ref, h_ref, g_ref, b_ref, o_ref, acc_ref):
    p = pl.program_id(0)
    flags = pf_ref[p]

    @pl.when((flags & PAIR_FIRST) != 0)
    def _():
        acc_ref[...] = jnp.zeros_like(acc_ref)

    @pl.when((flags & PAIR_VALID) != 0)
    def _():
        cols = pt_ref[p] * MOE_TILE + lax.broadcasted_iota(jnp.int32, (MOE_SRC_TILE, MOE_TILE), 1)
        onehot = jnp.where(pos_ref[...] == cols, 1.0, 0.0).astype(BF16)
        y_hi, y_lo = _split_bf16(y_ref[...])
        acc_ref[...] += _dot(onehot, y_hi) + _dot(onehot, y_lo)

    @pl.when((flags & PAIR_LAST) != 0)
    def _():
        o_ref[...] = _layer_norm(DEEPNORM_ALPHA * h_ref[...] + acc_ref[...], g_ref[...], b_ref[...])


def _moe_combine(plan, y_sorted, pos_col, h, ln_g, ln_b):
    tile, src, grp, flags = plan
    src_rows = lambda w: pl.BlockSpec((MOE_SRC_TILE, w), lambda p, pt, pr, pg, pf: (pr[p], 0))
    const = lambda shape: pl.BlockSpec(shape, lambda p, pt, pr, pg, pf: (0,) * len(shape))
    return pl.pallas_call(
        _moe_combine_kernel,
        grid_spec=pltpu.PrefetchScalarGridSpec(
            num_scalar_prefetch=4,
            grid=(MOE_MAX_PAIRS,),
            in_specs=[pl.BlockSpec((MOE_TILE, D_MODEL), lambda p, pt, pr, pg, pf: (pt[p], 0)),
                      src_rows(1), src_rows(D_MODEL), const((1, D_MODEL)), const((1, D_MODEL))],
            out_specs=src_rows(D_MODEL),
            scratch_shapes=[pltpu.VMEM((MOE_SRC_TILE, D_MODEL), F32)]),
        out_shape=jax.ShapeDtypeStruct((SEQ, D_MODEL), F32),
        compiler_params=pltpu.CompilerParams(dimension_semantics=("arbitrary",), vmem_limit_bytes=VMEM_LIMIT),
        name="moe_combine",
    )(tile, src, grp, flags, y_sorted, pos_col, h, ln_g, ln_b)


def _rope_tables():
    half = HEAD_DIM // 2
    inv_freq = ROPE_THETA ** (-jnp.arange(half, dtype=F32) / half)
    ang = jnp.arange(SEQ, dtype=F32)[:, None] * inv_freq[None, :]
    cos, sin = jnp.cos(ang), jnp.sin(ang)
    reps = LANES // HEAD_DIM
    return jnp.tile(jnp.concatenate([cos, cos], axis=1), (1, reps)), jnp.tile(jnp.concatenate([-sin, sin], axis=1), (1, reps))


def kernel(x, ln_in_g, ln_in_b, w_in, w_out, sg_ln_g, sg_ln_b, sg_w, sg_b, conv_w, conv_b, conv_ln_g, conv_ln_b, ln_mix_g, ln_mix_b, w_router, router_bias, w_gate, w_up, w_down, ln_ffn_g, ln_ffn_b):
    assert x.shape == (1, SEQ, D_MODEL) and w_in.shape == (DEPTH, D_MODEL, IN_COLS)
    cos, sin = _rope_tables()
    row = lambda a: a.reshape(1, -1)
    w_router_t = w_router.T
    rb = router_bias.reshape(N_EXPERTS, 1)
    h = x.reshape(SEQ, D_MODEL)
    for l in range(DEPTH):
        sg_bias = jnp.repeat(sg_b[l].T, SG_GROUP_DIM, axis=1)
        w_p = jnp.concatenate([w_in[l][:, :V_OFF], w_in[l][:, SGU_OFF:]], axis=1).astype(BF16)
        w_vt = w_in[l][:, V_OFF:SGU_OFF].T.astype(BF16)
        args = (w_p, w_vt, cos, sin, row(sg_ln_g[l]), row(sg_ln_b[l]), sg_w[l], sg_bias,
                conv_w[l], row(conv_b[l]), row(conv_ln_g[l]), row(conv_ln_b[l]))
        if l == 0:
            h, q, kx, vt, kmean, y_sg, y_cv = _front(True, h, row(ln_in_g), row(ln_in_b), *args)
        else:
            q, kx, vt, kmean, y_sg, y_cv = _front(False, h, None, None, *args)
        y_att = _attn(q, kx, vt, kmean[:, 0, :])
        h, h_bf16, grp, rank, wk = _outproj(h, y_att, y_sg, y_cv, w_out[l].astype(BF16), row(ln_mix_g[l]),
                                            row(ln_mix_b[l]), w_router_t, rb)
        pos, by_tile, by_src = _moe_plan(grp[0], rank[0])
        y_sorted = _moe_ffn(by_tile, h_bf16, pos.reshape(1, SEQ), wk, w_gate[l].astype(BF16),
                            w_up[l].astype(BF16), w_down[l].astype(BF16))
        h = _moe_combine(by_src, y_sorted, pos.reshape(SEQ, 1), h, row(ln_ffn_g[l]), row(ln_ffn_b[l]))
    return h.reshape(1, SEQ, D_MODEL)
```

```python
import functools
import math

import jax
import jax.numpy as jnp
from jax import lax
from jax.experimental import pallas as pl
from jax.experimental.pallas import tpu as pltpu

D_MODEL = 1024
SEQ = 16384
DEPTH = 2
HEAD_DIM = 64
ATT_WIDTH = 512
ATT_HEADS = 8
SG_WIDTH = 256
SG_GROUPS = 4
SG_GROUP_DIM = 64
CONV_WIDTH = 256
K_OFF = 512
V_OFF = 1024
SGU_OFF = 1536
SGV_OFF = 1792
CA_OFF = 2048
CG_OFF = 2304
IN_COLS = 2560
P_SGU = SGU_OFF - ATT_WIDTH
P_SGV = SGV_OFF - ATT_WIDTH
P_CA = CA_OFF - ATT_WIDTH
P_CG = CG_OFF - ATT_WIDTH
P_COLS = IN_COLS - ATT_WIDTH
MOBA_BLOCK = 256
MOBA_TOPK = 3
N_BLOCKS = SEQ // MOBA_BLOCK
SG_CHUNK = 128
CONV_TAPS = 31
ROPE_THETA = 10000.0
N_EXPERTS = 16
N_EXPERT_GROUPS = 4
EXPERTS_PER_GROUP = 4
D_EXPERT = 512
LN_EPS = 1e-5
NEG = -1e30
DEEPNORM_ALPHA = (2.0 * DEPTH) ** 0.25

LANES = 128
SUBLANES = 8
ROW_TILE = MOBA_BLOCK
CONV_HALO = 32
LOG2_E = math.log2(math.e)
VT_ROWS = HEAD_DIM + 16
ATT_GROUP_SHIFT = 2
ATT_GROUP = 1 << ATT_GROUP_SHIFT
ATT_Q_BLOCKS = 2
ATT_Q_TILE = ATT_Q_BLOCKS * MOBA_BLOCK
ATT_GROUP_KEYS = ATT_GROUP * MOBA_BLOCK
MOE_TILE = 256
MOE_SRC_TILE = 512
MOE_MAX_TILES = SEQ // MOE_TILE + N_EXPERT_GROUPS
MOE_MAX_PAIRS = MOE_MAX_TILES + N_EXPERT_GROUPS * (SEQ // MOE_SRC_TILE)
MOE_SORTED_ROWS = MOE_MAX_TILES * MOE_TILE
VMEM_LIMIT = 48 * 1024 * 1024

F32 = jnp.float32
BF16 = jnp.bfloat16


def _dot(a, b):
    return jnp.dot(a, b, preferred_element_type=F32)


def _dot_nt(a, b):
    return lax.dot_general(a, b, (((1,), (1,)), ((), ())), preferred_element_type=F32)


def _layer_norm(x, g, b):
    mu = jnp.mean(x, axis=-1, keepdims=True)
    xc = x - mu
    var = jnp.mean(xc * xc, axis=-1, keepdims=True)
    return xc * lax.rsqrt(var + LN_EPS) * g + b


def _split_bf16(x):
    hi = x.astype(BF16)
    lo = (x - hi.astype(F32)).astype(BF16)
    return hi, lo


def _front_kernel(apply_ln, *refs):
    if apply_ln:
        (x_ref, lng_ref, lnb_ref, w_ref, wvt_ref, cos_ref, sin_ref, sglng_ref, sglnb_ref, sgw_ref, sgb_ref,
         cw_ref, cb_ref, clng_ref, clnb_ref,
         h_ref, q_ref, kx_ref, vt_ref, km_ref, ysg_ref, ycv_ref, ext_ref) = refs
    else:
        (x_ref, w_ref, wvt_ref, cos_ref, sin_ref, sglng_ref, sglnb_ref, sgw_ref, sgb_ref,
         cw_ref, cb_ref, clng_ref, clnb_ref,
         q_ref, kx_ref, vt_ref, km_ref, ysg_ref, ycv_ref, ext_ref) = refs
    i = pl.program_id(0)
    h = x_ref[...]
    if apply_ln:
        h = _layer_norm(h, lng_ref[...], lnb_ref[...])
        h_ref[...] = h
    hb = h.astype(BF16)
    p = _dot(hb, w_ref[...])
    vt_ref[:, 0:HEAD_DIM, :] = _dot_nt(wvt_ref[...], hb).reshape(ATT_HEADS, HEAD_DIM, ROW_TILE).astype(BF16)
    vt_ref[:, HEAD_DIM:, :] = jnp.ones((ATT_HEADS, VT_ROWS - HEAD_DIM, ROW_TILE), BF16)

    lane = lax.broadcasted_iota(jnp.int32, (ROW_TILE, LANES), 1)
    first_half = (lane % HEAD_DIM) < (HEAD_DIM // 2)
    low_head = lane < HEAD_DIM
    cos = cos_ref[...]
    sin = sin_ref[...]

    def rot(x):
        swapped = jnp.where(first_half, pltpu.roll(x, LANES - HEAD_DIM // 2, 1), pltpu.roll(x, HEAD_DIM // 2, 1))
        return x * cos + swapped * sin

    scale = HEAD_DIM ** -0.5 * LOG2_E
    for j in range(ATT_WIDTH // LANES):
        sl = slice(j * LANES, (j + 1) * LANES)
        q_ref[:, sl] = (rot(p[:, sl]) * scale).astype(BF16)
        kr = rot(p[:, K_OFF + j * LANES:K_OFF + (j + 1) * LANES])
        km_ref[0, :, sl] = jnp.broadcast_to(jnp.mean(kr, axis=0, keepdims=True), (SUBLANES, LANES))
        kx_ref[2 * j] = jnp.where(low_head, kr, (lane - HEAD_DIM == i).astype(F32)).astype(BF16)
        kx_ref[2 * j + 1] = jnp.where(low_head, (lane == i).astype(F32), kr).astype(BF16)

    u = jax.nn.gelu(p[:, P_SGU:P_SGV])
    vv = _layer_norm(jax.nn.gelu(p[:, P_SGV:P_CA]), sglng_ref[...], sglnb_ref[...]).astype(BF16)
    r_i = lax.broadcasted_iota(jnp.int32, (SG_CHUNK, SG_CHUNK), 0)
    c_i = lax.broadcasted_iota(jnp.int32, (SG_CHUNK, SG_CHUNK), 1)
    lane_sg = lax.broadcasted_iota(jnp.int32, (SG_CHUNK, SG_WIDTH), 1)
    for c in range(ROW_TILE // SG_CHUNK):
        rows = slice(c * SG_CHUNK, (c + 1) * SG_CHUNK)
        mixed = sgb_ref[...]
        for g in range(SG_GROUPS):
            wg = jnp.where(c_i <= r_i, sgw_ref[g], 0.0).astype(BF16)
            full = _dot(wg, vv[rows])
            mixed = mixed + jnp.where(lane_sg // SG_GROUP_DIM == g, full, 0.0)
        ysg_ref[rows, :] = (u[rows] * mixed).astype(BF16)

    @pl.when(i == 0)
    def _():
        ext_ref[0:CONV_HALO, :] = jnp.zeros((CONV_HALO, CONV_WIDTH), F32)

    ext_ref[CONV_HALO:, :] = p[:, P_CA:P_CG] * jax.nn.sigmoid(p[:, P_CG:P_COLS])
    acc = jnp.zeros((ROW_TILE, CONV_WIDTH), F32) + cb_ref[...]
    first = CONV_HALO - (CONV_TAPS - 1)
    ext = ext_ref[...]
    n_ext = CONV_HALO + ROW_TILE
    for shift in range(SUBLANES):
        shifted = ext if shift == 0 else pltpu.roll(ext, n_ext - shift, 0)
        for o in range(first + (shift - first) % SUBLANES, first + CONV_TAPS, SUBLANES):
            acc = acc + cw_ref[o - first:o - first + 1, :] * shifted[o - shift:o - shift + ROW_TILE]
    ext_ref[0:CONV_HALO, :] = ext_ref[ROW_TILE:ROW_TILE + CONV_HALO, :]
    hc = _layer_norm(acc, clng_ref[...], clnb_ref[...])
    ycv_ref[...] = (hc * jax.nn.sigmoid(hc)).astype(BF16)


def _front(apply_ln, x, ln_g, ln_b, w_p, w_vt, cos, sin, sg_ln_g, sg_ln_b, sg_w, sg_bias, conv_w, conv_b,
           conv_ln_g, conv_ln_b):
    n_tiles = SEQ // ROW_TILE
    row = lambda w: pl.BlockSpec((ROW_TILE, w), lambda i: (i, 0))
    full = lambda shape: pl.BlockSpec(shape, lambda i: (0,) * len(shape))
    in_specs = [row(D_MODEL)]
    args = [x]
    if apply_ln:
        in_specs += [full((1, D_MODEL)), full((1, D_MODEL))]
        args += [ln_g, ln_b]
    in_specs += [full((D_MODEL, P_COLS)), full((ATT_WIDTH, D_MODEL)), row(LANES), row(LANES),
                 full((1, SG_WIDTH)), full((1, SG_WIDTH)),
                 full((SG_GROUPS, SG_CHUNK, SG_CHUNK)), full((SG_CHUNK, SG_WIDTH)),
                 full((CONV_TAPS, CONV_WIDTH)), full((1, CONV_WIDTH)), full((1, CONV_WIDTH)),
                 full((1, CONV_WIDTH))]
    args += [w_p, w_vt, cos, sin, sg_ln_g, sg_ln_b, sg_w, sg_bias, conv_w, conv_b, conv_ln_g, conv_ln_b]
    out_shape = [jax.ShapeDtypeStruct((SEQ, ATT_WIDTH), BF16),
                 jax.ShapeDtypeStruct((ATT_HEADS, SEQ, LANES), BF16),
                 jax.ShapeDtypeStruct((ATT_HEADS, VT_ROWS, SEQ), BF16),
                 jax.ShapeDtypeStruct((N_BLOCKS, SUBLANES, ATT_WIDTH), F32),
                 jax.ShapeDtypeStruct((SEQ, SG_WIDTH), BF16),
                 jax.ShapeDtypeStruct((SEQ, CONV_WIDTH), BF16)]
    out_specs = [row(ATT_WIDTH), pl.BlockSpec((ATT_HEADS, ROW_TILE, LANES), lambda i: (0, i, 0)),
                 pl.BlockSpec((ATT_HEADS, VT_ROWS, ROW_TILE), lambda i: (0, 0, i)),
                 pl.BlockSpec((1, SUBLANES, ATT_WIDTH), lambda i: (i, 0, 0)),
                 row(SG_WIDTH), row(CONV_WIDTH)]
    if apply_ln:
        out_shape = [jax.ShapeDtypeStruct((SEQ, D_MODEL), F32)] + out_shape
        out_specs = [row(D_MODEL)] + out_specs
    return pl.pallas_call(
        functools.partial(_front_kernel, apply_ln),
        grid=(n_tiles,),
        in_specs=in_specs,
        out_specs=out_specs,
        out_shape=out_shape,
        scratch_shapes=[pltpu.VMEM((CONV_HALO + ROW_TILE, CONV_WIDTH), F32)],
        compiler_params=pltpu.CompilerParams(dimension_semantics=("arbitrary",), vmem_limit_bytes=VMEM_LIMIT),
        name="front_ln" if apply_ln else "front",
    )(*args)


def _attn_kernel(q_ref, kx_ref, vt_ref, km_ref, o_ref, s0_ref, s1_ref, mx0_ref, mx1_ref):
    first_blk = pl.program_id(1) * ATT_Q_BLOCKS
    q = q_ref[...]

    km = km_ref[...]
    km_lane = lax.broadcasted_iota(jnp.int32, (N_BLOCKS, LANES), 1)
    km_rows = jnp.concatenate([jnp.where(km_lane >= HEAD_DIM, km, 0.0), jnp.where(km_lane < HEAD_DIM, km, 0.0)],
                              axis=0)
    km_hi, km_lo = _split_bf16(km_rows)
    eye = (lax.broadcasted_iota(jnp.int32, (LANES, LANES), 0)
           == lax.broadcasted_iota(jnp.int32, (LANES, LANES), 1)).astype(F32).astype(BF16)
    res = _dot_nt(jnp.concatenate([km_hi, km_lo, eye], axis=0), q)
    q_t = res[2 * LANES:].astype(BF16)
    gate = (res[0:LANES] + res[LANES:2 * LANES]).reshape(2, N_BLOCKS, ATT_Q_TILE)
    blk = lax.broadcasted_iota(jnp.int32, (2, N_BLOCKS, ATT_Q_TILE), 1)
    own = first_blk + lax.broadcasted_iota(jnp.int32, (2, N_BLOCKS, ATT_Q_TILE), 2) // MOBA_BLOCK
    gate = jnp.where(blk < own, gate, NEG)

    sel = jnp.zeros((2, N_BLOCKS, ATT_Q_TILE), jnp.bool_)
    blk_f = blk.astype(F32)
    for r in range(MOBA_TOPK):
        best = jnp.max(gate, axis=1, keepdims=True)
        first = jnp.min(jnp.where(gate == best, blk_f, float(N_BLOCKS)), axis=1, keepdims=True)
        pick = blk_f == first
        sel = sel | (pick & (own > r))
        gate = jnp.where(pick, -jnp.inf, gate)
    bias_t = jnp.where(sel, 0.0, NEG).astype(BF16).reshape(LANES, ATT_Q_TILE)
    low = lax.broadcasted_iota(jnp.int32, (LANES, ATT_Q_TILE), 0) < HEAD_DIM
    zero = jnp.zeros_like(q_t)
    qx_past = (jnp.where(low, q_t, bias_t), jnp.where(low, bias_t, q_t))
    qx_own = (jnp.where(low, q_t, zero), jnp.where(low, zero, q_t))

    key_i = lax.broadcasted_iota(jnp.int32, (ATT_Q_TILE, ATT_Q_TILE), 0)
    qry_i = lax.broadcasted_iota(jnp.int32, (ATT_Q_TILE, ATT_Q_TILE), 1)
    own_mask = (key_i <= qry_i) & (key_i // MOBA_BLOCK == qry_i // MOBA_BLOCK)
    start_own = pl.multiple_of(first_blk * MOBA_BLOCK, ATT_Q_TILE)

    state = []
    for par in range(2):
        s = _dot(kx_ref[par, pl.ds(start_own, ATT_Q_TILE), :], qx_own[par])
        s = jnp.where(own_mask, s, NEG)
        m = jnp.max(s, axis=0, keepdims=True)
        p = jnp.exp2(s - m)
        acc = _dot(vt_ref[par, :, pl.ds(start_own, ATT_Q_TILE)], p.astype(BF16))
        state += [m, acc]

    n_groups = lax.shift_right_logical(first_blk + (ATT_Q_BLOCKS - 1) + (ATT_GROUP - 1), ATT_GROUP_SHIFT)
    last_group = N_BLOCKS // ATT_GROUP - 1

    def scores(g, s_ref, mx_ref):
        start = pl.multiple_of(g * ATT_GROUP_KEYS, ATT_GROUP_KEYS)
        for par in range(2):
            s = _dot(kx_ref[par, pl.ds(start, ATT_GROUP_KEYS), :], qx_past[par])
            s_ref[par] = s
            mx_ref[par] = jnp.max(s, axis=0, keepdims=True)

    def consume(g, s_ref, mx_ref, carry):
        start = pl.multiple_of(g * ATT_GROUP_KEYS, ATT_GROUP_KEYS)
        out = []
        for par in range(2):
            m, acc = carry[2 * par:2 * par + 2]
            m_new = jnp.maximum(m, mx_ref[par])
            a = jnp.exp2(m - m_new)
            p = jnp.exp2(s_ref[par] - m_new)
            acc = a * acc + _dot(vt_ref[par, :, pl.ds(start, ATT_GROUP_KEYS)], p.astype(BF16))
            out += [m_new, acc]
        return tuple(out)

    scores(0, s0_ref, mx0_ref)

    def body(g, carry):
        g_next = jnp.minimum(g + 1, last_group)

        def even(c):
            scores(g_next, s1_ref, mx1_ref)
            return consume(g, s0_ref, mx0_ref, c)

        def odd(c):
            scores(g_next, s0_ref, mx0_ref)
            return consume(g, s1_ref, mx1_ref, c)

        return lax.cond((g & 1) == 0, even, odd, carry)

    state = lax.fori_loop(0, n_groups, body, tuple(state))
    o_t = jnp.concatenate([acc[0:HEAD_DIM] / acc[HEAD_DIM:HEAD_DIM + 1] for acc in (state[1], state[3])],
                          axis=0)
    o_ref[...] = o_t.T.astype(BF16)


def _attn(q, kx, vt, kmean):
    n_pairs = ATT_WIDTH // LANES
    return pl.pallas_call(
        _attn_kernel,
        grid=(n_pairs, SEQ // ATT_Q_TILE),
        in_specs=[pl.BlockSpec((ATT_Q_TILE, LANES), lambda j, t: (t, j)),
                  pl.BlockSpec((2, SEQ, LANES), lambda j, t: (j, 0, 0)),
                  pl.BlockSpec((2, VT_ROWS, SEQ), lambda j, t: (j, 0, 0)),
                  pl.BlockSpec((N_BLOCKS, LANES), lambda j, t: (0, j))],
        out_specs=pl.BlockSpec((ATT_Q_TILE, LANES), lambda j, t: (t, j)),
        out_shape=jax.ShapeDtypeStruct((SEQ, ATT_WIDTH), BF16),
        scratch_shapes=([pltpu.VMEM((2, ATT_GROUP_KEYS, ATT_Q_TILE), F32)] * 2
                        + [pltpu.VMEM((2, 1, ATT_Q_TILE), F32)] * 2),
        compiler_params=pltpu.CompilerParams(dimension_semantics=("arbitrary", "arbitrary"),
                                             vmem_limit_bytes=VMEM_LIMIT),
        name="attn",
    )(q, kx, vt, kmean)


def _outproj_kernel(h_ref, ya_ref, ys_ref, yc_ref, wo_ref, g_ref, b_ref, wr_ref, rb_ref,
                    h1_ref, h1b_ref, grp_ref, rank_ref, wk_ref, cnt_ref):
    mix = (_dot(ya_ref[...], wo_ref[0:ATT_WIDTH, :])
           + _dot(ys_ref[...], wo_ref[ATT_WIDTH:ATT_WIDTH + SG_WIDTH, :])
           + _dot(yc_ref[...], wo_ref[ATT_WIDTH + SG_WIDTH:, :]))
    h1 = _layer_norm(DEEPNORM_ALPHA * h_ref[...] + mix, g_ref[...], b_ref[...])
    h1_ref[...] = h1
    h1b_ref[...] = h1.astype(BF16)

    w_hi, w_lo = _split_bf16(wr_ref[...])
    h_hi, h_lo = _split_bf16(h1)
    both = _dot_nt(jnp.concatenate([w_hi, w_lo], axis=0), h_hi)
    logits = both[0:N_EXPERTS] + both[N_EXPERTS:] + _dot_nt(w_hi, h_lo)
    scores = jax.nn.sigmoid(logits)
    biased = scores + rb_ref[...]
    sc = [scores[e:e + 1, :] for e in range(N_EXPERTS)]
    bi = [biased[e:e + 1, :] for e in range(N_EXPERTS)]

    best = None
    for g in range(N_EXPERT_GROUPS):
        a0, a1, a2, a3 = bi[4 * g:4 * g + 4]
        hi01, lo01 = jnp.maximum(a0, a1), jnp.minimum(a0, a1)
        hi23, lo23 = jnp.maximum(a2, a3), jnp.minimum(a2, a3)
        top1 = jnp.maximum(hi01, hi23)
        top2 = jnp.maximum(jnp.minimum(hi01, hi23), jnp.maximum(lo01, lo23))
        gs = top1 + top2
        if best is None:
            best, grp = gs, jnp.zeros_like(gs, dtype=jnp.int32)
        else:
            upd = gs > best
            best = jnp.where(upd, gs, best)
            grp = jnp.where(upd, g, grp)

    def in_group(vals, k):
        out = vals[k]
        for g in range(1, N_EXPERT_GROUPS):
            out = jnp.where(grp == g, vals[4 * g + k], out)
        return out

    vb = [in_group(bi, k) for k in range(EXPERTS_PER_GROUP)]
    vs = [in_group(sc, k) for k in range(EXPERTS_PER_GROUP)]
    m1, i1, s1 = vb[0], jnp.zeros_like(grp), vs[0]
    for k in range(1, EXPERTS_PER_GROUP):
        upd = vb[k] > m1
        m1 = jnp.where(upd, vb[k], m1)
        i1 = jnp.where(upd, k, i1)
        s1 = jnp.where(upd, vs[k], s1)
    m2 = jnp.full_like(m1, -jnp.inf)
    i2 = jnp.zeros_like(grp)
    s2 = jnp.zeros_like(s1)
    for k in range(EXPERTS_PER_GROUP):
        upd = (i1 != k) & (vb[k] > m2)
        m2 = jnp.where(upd, vb[k], m2)
        i2 = jnp.where(upd, k, i2)
        s2 = jnp.where(upd, vs[k], s2)
    tot = s1 + s2
    k_idx = lax.broadcasted_iota(jnp.int32, (EXPERTS_PER_GROUP, ROW_TILE), 0)
    wk_ref[...] = jnp.where(k_idx == i1, s1 / tot, 0.0) + jnp.where(k_idx == i2, s2 / tot, 0.0)
    grp_ref[...] = grp

    @pl.when(pl.program_id(0) == 0)
    def _():
        cnt_ref[...] = jnp.zeros_like(cnt_ref)

    member = (k_idx == grp).astype(F32)
    t_r = lax.broadcasted_iota(jnp.int32, (ROW_TILE, ROW_TILE), 0)
    t_c = lax.broadcasted_iota(jnp.int32, (ROW_TILE, ROW_TILE), 1)
    earlier = _dot(member.astype(BF16), (t_r < t_c).astype(BF16)) + cnt_ref[...]
    rank_ref[...] = jnp.sum(member * earlier, axis=0, keepdims=True).astype(jnp.int32)
    cnt_ref[...] += jnp.sum(member, axis=1, keepdims=True)


def _outproj(h, y_att, y_sg, y_cv, w_out, ln_g, ln_b, w_router_t, router_bias):
    row = lambda w: pl.BlockSpec((ROW_TILE, w), lambda i: (i, 0))
    full = lambda shape: pl.BlockSpec(shape, lambda i: (0,) * len(shape))
    lane_row = lambda n: pl.BlockSpec((n, ROW_TILE), lambda i: (0, i))
    return pl.pallas_call(
        _outproj_kernel,
        grid=(SEQ // ROW_TILE,),
        in_specs=[row(D_MODEL), row(ATT_WIDTH), row(SG_WIDTH), row(CONV_WIDTH), full((D_MODEL, D_MODEL)),
                  full((1, D_MODEL)), full((1, D_MODEL)), full((N_EXPERTS, D_MODEL)), full((N_EXPERTS, 1))],
        out_specs=[row(D_MODEL), row(D_MODEL), lane_row(1), lane_row(1), lane_row(EXPERTS_PER_GROUP)],
        out_shape=[jax.ShapeDtypeStruct((SEQ, D_MODEL), F32),
                   jax.ShapeDtypeStruct((SEQ, D_MODEL), BF16),
                   jax.ShapeDtypeStruct((1, SEQ), jnp.int32),
                   jax.ShapeDtypeStruct((1, SEQ), jnp.int32),
                   jax.ShapeDtypeStruct((EXPERTS_PER_GROUP, SEQ), F32)],
        scratch_shapes=[pltpu.VMEM((N_EXPERT_GROUPS, ROW_TILE), F32)],
        compiler_params=pltpu.CompilerParams(dimension_semantics=("arbitrary",), vmem_limit_bytes=VMEM_LIMIT),
        name="outproj",
    )(h, y_att, y_sg, y_cv, w_out, ln_g, ln_b, w_router_t, router_bias)


PAIR_VALID, PAIR_FIRST, PAIR_LAST = 1, 2, 4


def _moe_plan(grp, rank):
    n_src = SEQ // MOE_SRC_TILE
    member = (grp[:, None] == jnp.arange(N_EXPERT_GROUPS)[None, :]).astype(jnp.int32)
    cnt = member.reshape(n_src, MOE_SRC_TILE, N_EXPERT_GROUPS).sum(axis=1).T
    start = jnp.cumsum(cnt, axis=1) - cnt
    tiles = (cnt.sum(axis=1) + MOE_TILE - 1) // MOE_TILE
    tile0 = jnp.cumsum(tiles) - tiles
    pos = tile0[grp] * MOE_TILE + rank

    lo = jnp.arange(SEQ // MOE_TILE)[None, :, None] * MOE_TILE
    overlap = (cnt[:, None, :] > 0) & (start[:, None, :] < lo + MOE_TILE) & ((start + cnt)[:, None, :] > lo)
    n_pairs = overlap.sum()
    idx = jnp.arange(MOE_MAX_PAIRS)
    keep = jnp.minimum(idx, n_pairs - 1)

    def pair_list(mask, order):
        running = jnp.cumsum(mask.reshape(-1).astype(jnp.int32))
        flat = (running[None, :] <= keep[:, None]).sum(axis=1)
        ix = jnp.unravel_index(flat, mask.shape)
        g, il, r = (ix[order.index(a)] for a in "gtr")
        tile = tile0[g] + il
        key = tile if order[0] == "g" else r
        valid = idx < n_pairs
        first = valid & ((idx == 0) | (key != jnp.roll(key, 1)))
        last = valid & ((idx == n_pairs - 1) | (key != jnp.roll(key, -1)))
        flags = PAIR_VALID * valid + PAIR_FIRST * first + PAIR_LAST * last
        return tile.astype(jnp.int32), r.astype(jnp.int32), g.astype(jnp.int32), flags.astype(jnp.int32)

    return pos, pair_list(overlap, "gtr"), pair_list(overlap.transpose(2, 0, 1), "rgt")


def _moe_ffn_kernel(pt_ref, pr_ref, pg_ref, pf_ref, h_ref, pos_ref, wk_ref, wg_ref, wu_ref, wd_ref, y_ref,
                    x_ref, w_ref):
    p = pl.program_id(0)
    flags = pf_ref[p]

    @pl.when((flags & PAIR_FIRST) != 0)
    def _():
        x_ref[...] = jnp.zeros_like(x_ref)
        w_ref[...] = jnp.zeros_like(w_ref)

    @pl.when((flags & PAIR_VALID) != 0)
    def _():
        rows = pt_ref[p] * MOE_TILE + lax.broadcasted_iota(jnp.int32, (MOE_TILE, MOE_SRC_TILE), 0)
        onehot = jnp.where(pos_ref[...] == rows, 1.0, 0.0).astype(BF16)
        x_ref[...] += _dot(onehot, h_ref[...])
        wk = wk_ref[...]
        for _ in range(3):
            piece = wk.astype(BF16)
            w_ref[...] += _dot_nt(onehot, piece)
            wk = wk - piece.astype(F32)

    @pl.when((flags & PAIR_LAST) != 0)
    def _():
        x = x_ref[...].astype(BF16)
        w = w_ref[...]
        y = jnp.zeros((MOE_TILE, D_MODEL), F32)
        for k in range(EXPERTS_PER_GROUP):
            a = jax.nn.silu(_dot(x, wg_ref[k])) * _dot(x, wu_ref[k])
            y = y + w[:, k:k + 1] * _dot(a.astype(BF16), wd_ref[k])
        y_ref[...] = y


def _moe_ffn(plan, h_bf16, pos, wk, w_gate, w_up, w_down):
    tile, src, grp, flags = plan
    src_rows = lambda w: pl.BlockSpec((MOE_SRC_TILE, w), lambda p, pt, pr, pg, pf: (pr[p], 0))
    src_lanes = lambda n: pl.BlockSpec((n, MOE_SRC_TILE), lambda p, pt, pr, pg, pf: (0, pr[p]))
    experts = lambda a, b: pl.BlockSpec((EXPERTS_PER_GROUP, a, b), lambda p, pt, pr, pg, pf: (pg[p], 0, 0))
    return pl.pallas_call(
        _moe_ffn_kernel,
        grid_spec=pltpu.PrefetchScalarGridSpec(
            num_scalar_prefetch=4,
            grid=(MOE_MAX_PAIRS,),
            in_specs=[src_rows(D_MODEL), src_lanes(1), src_lanes(EXPERTS_PER_GROUP),
                      experts(D_MODEL, D_EXPERT), experts(D_MODEL, D_EXPERT), experts(D_EXPERT, D_MODEL)],
            out_specs=pl.BlockSpec((MOE_TILE, D_MODEL), lambda p, pt, pr, pg, pf: (pt[p], 0)),
            scratch_shapes=[pltpu.VMEM((MOE_TILE, D_MODEL), F32), pltpu.VMEM((MOE_TILE, EXPERTS_PER_GROUP), F32)]),
        out_shape=jax.ShapeDtypeStruct((MOE_SORTED_ROWS, D_MODEL), F32),
        compiler_params=pltpu.CompilerParams(dimension_semantics=("arbitrary",), vmem_limit_bytes=VMEM_LIMIT),
        name="moe_ffn",
    )(tile, src, grp, flags, h_bf16, pos, wk, w_gate, w_up, w_down)


def _moe_combine_kernel(pt_ref, pr_ref, pg_ref, pf_ref, y_ref, pos_ref, h_ref, g_ref, b_ref, o_ref, acc_ref):
    p = pl.program_id(0)
    flags = pf_ref[p]

    @pl.when((flags & PAIR_FIRST) != 0)
    def _():
        acc_ref[...] = jnp.zeros_like(acc_ref)

    @pl.when((flags & PAIR_VALID) != 0)
    def _():
        cols = pt_ref[p] * MOE_TILE + lax.broadcasted_iota(jnp.int32, (MOE_SRC_TILE, MOE_TILE), 1)
        onehot = jnp.where(pos_ref[...] == cols, 1.0, 0.0).astype(BF16)
        y_hi, y_lo = _split_bf16(y_ref[...])
        acc_ref[...] += _dot(onehot, y_hi) + _dot(onehot, y_lo)

    @pl.when((flags & PAIR_LAST) != 0)
    def _():
        o_ref[...] = _layer_norm(DEEPNORM_ALPHA * h_ref[...] + acc_ref[...], g_ref[...], b_ref[...])


def _moe_combine(plan, y_sorted, pos_col, h, ln_g, ln_b):
    tile, src, grp, flags = plan
    src_rows = lambda w: pl.BlockSpec((MOE_SRC_TILE, w), lambda p, pt, pr, pg, pf: (pr[p], 0))
    const = lambda shape: pl.BlockSpec(shape, lambda p, pt, pr, pg, pf: (0,) * len(shape))
    return pl.pallas_call(
        _moe_combine_kernel,
        grid_spec=pltpu.PrefetchScalarGridSpec(
            num_scalar_prefetch=4,
            grid=(MOE_MAX_PAIRS,),
            in_specs=[pl.BlockSpec((MOE_TILE, D_MODEL), lambda p, pt, pr, pg, pf: (pt[p], 0)),
                      src_rows(1), src_rows(D_MODEL), const((1, D_MODEL)), const((1, D_MODEL))],
            out_specs=src_rows(D_MODEL),
            scratch_shapes=[pltpu.VMEM((MOE_SRC_TILE, D_MODEL), F32)]),
        out_shape=jax.ShapeDtypeStruct((SEQ, D_MODEL), F32),
        compiler_params=pltpu.CompilerParams(dimension_semantics=("arbitrary",), vmem_limit_bytes=VMEM_LIMIT),
        name="moe_combine",
    )(tile, src, grp, flags, y_sorted, pos_col, h, ln_g, ln_b)


def _rope_tables():
    half = HEAD_DIM // 2
    inv_freq = ROPE_THETA ** (-jnp.arange(half, dtype=F32) / half)
    ang = jnp.arange(SEQ, dtype=F32)[:, None] * inv_freq[None, :]
    cos, sin = jnp.cos(ang), jnp.sin(ang)
    reps = LANES // HEAD_DIM
    return jnp.tile(jnp.concatenate([cos, cos], axis=1), (1, reps)), jnp.tile(jnp.concatenate([-sin, sin], axis=1), (1, reps))


def kernel(x, ln_in_g, ln_in_b, w_in, w_out, sg_ln_g, sg_ln_b, sg_w, sg_b, conv_w, conv_b, conv_ln_g, conv_ln_b, ln_mix_g, ln_mix_b, w_router, router_bias, w_gate, w_up, w_down, ln_ffn_g, ln_ffn_b):
    assert x.shape == (1, SEQ, D_MODEL) and w_in.shape == (DEPTH, D_MODEL, IN_COLS)
    cos, sin = _rope_tables()
    row = lambda a: a.reshape(1, -1)
    w_router_t = w_router.T
    rb = router_bias.reshape(N_EXPERTS, 1)
    h = x.reshape(SEQ, D_MODEL)
    for l in range(DEPTH):
        sg_bias = jnp.repeat(sg_b[l].T, SG_GROUP_DIM, axis=1)
        w_p = jnp.concatenate([w_in[l][:, :V_OFF], w_in[l][:, SGU_OFF:]], axis=1).astype(BF16)
        w_vt = w_in[l][:, V_OFF:SGU_OFF].T.astype(BF16)
        args = (w_p, w_vt, cos, sin, row(sg_ln_g[l]), row(sg_ln_b[l]), sg_w[l], sg_bias,
                conv_w[l], row(conv_b[l]), row(conv_ln_g[l]), row(conv_ln_b[l]))
        if l == 0:
            h, q, kx, vt, kmean, y_sg, y_cv = _front(True, h, row(ln_in_g), row(ln_in_b), *args)
        else:
            q, kx, vt, kmean, y_sg, y_cv = _front(False, h, None, None, *args)
        y_att = _attn(q, kx, vt, kmean[:, 0, :])
        h, h_bf16, grp, rank, wk = _outproj(h, y_att, y_sg, y_cv, w_out[l].astype(BF16), row(ln_mix_g[l]),
                                            row(ln_mix_b[l]), w_router_t, rb)
        pos, by_tile, by_src = _moe_plan(grp[0], rank[0])
        y_sorted = _moe_ffn(by_tile, h_bf16, pos.reshape(1, SEQ), wk, w_gate[l].astype(BF16),
                            w_up[l].astype(BF16), w_down[l].astype(BF16))
        h = _moe_combine(by_src, y_sorted, pos.reshape(SEQ, 1), h, row(ln_ffn_g[l]), row(ln_ffn_b[l]))
    return h.reshape(1, SEQ, D_MODEL)
```

```python
import functools
import math

import jax
import jax.numpy as jnp
from jax import lax
from jax.experimental import pallas as pl
from jax.experimental.pallas import tpu as pltpu

D_MODEL = 1024
SEQ = 16384
DEPTH = 2
HEAD_DIM = 64
ATT_WIDTH = 512
ATT_HEADS = 8
SG_WIDTH = 256
SG_GROUPS = 4
SG_GROUP_DIM = 64
CONV_WIDTH = 256
K_OFF = 512
V_OFF = 1024
SGU_OFF = 1536
SGV_OFF = 1792
CA_OFF = 2048
CG_OFF = 2304
IN_COLS = 2560
P_SGU = SGU_OFF - ATT_WIDTH
P_SGV = SGV_OFF - ATT_WIDTH
P_CA = CA_OFF - ATT_WIDTH
P_CG = CG_OFF - ATT_WIDTH
P_COLS = IN_COLS - ATT_WIDTH
MOBA_BLOCK = 256
MOBA_TOPK = 3
N_BLOCKS = SEQ // MOBA_BLOCK
SG_CHUNK = 128
CONV_TAPS = 31
ROPE_THETA = 10000.0
N_EXPERTS = 16
N_EXPERT_GROUPS = 4
EXPERTS_PER_GROUP = 4
D_EXPERT = 512
LN_EPS = 1e-5
NEG = -1e30
DEEPNORM_ALPHA = (2.0 * DEPTH) ** 0.25

LANES = 128
SUBLANES = 8
ROW_TILE = MOBA_BLOCK
CONV_HALO = 32
LOG2_E = math.log2(math.e)
VT_ROWS = HEAD_DIM + 16
ATT_GROUP_SHIFT = 2
ATT_GROUP = 1 << ATT_GROUP_SHIFT
ATT_Q_BLOCKS = 2
ATT_Q_TILE = ATT_Q_BLOCKS * MOBA_BLOCK
ATT_GROUP_KEYS = ATT_GROUP * MOBA_BLOCK
GATE_PIECES = 3
GATE_ROWS = 16
MOE_TILE = 256
MOE_SRC_TILE = 512
MOE_MAX_TILES = SEQ // MOE_TILE + N_EXPERT_GROUPS
MOE_MAX_PAIRS = MOE_MAX_TILES + N_EXPERT_GROUPS * (SEQ // MOE_SRC_TILE)
MOE_SORTED_ROWS = MOE_MAX_TILES * MOE_TILE
VMEM_LIMIT = 48 * 1024 * 1024

F32 = jnp.float32
BF16 = jnp.bfloat16


def _dot(a, b):
    return jnp.dot(a, b, preferred_element_type=F32)


def _dot_nt(a, b):
    return lax.dot_general(a, b, (((1,), (1,)), ((), ())), preferred_element_type=F32)


def _layer_norm(x, g, b):
    mu = jnp.mean(x, axis=-1, keepdims=True)
    xc = x - mu
    var = jnp.mean(xc * xc, axis=-1, keepdims=True)
    return xc * lax.rsqrt(var + LN_EPS) * g + b


def _split_bf16(x):
    hi = x.astype(BF16)
    lo = (x - hi.astype(F32)).astype(BF16)
    return hi, lo


def _front_kernel(apply_ln, *refs):
    if apply_ln:
        (x_ref, lng_ref, lnb_ref, w_ref, wvt_ref, cos_ref, sin_ref, sglng_ref, sglnb_ref, sgw_ref, sgb_ref,
         cw_ref, cb_ref, clng_ref, clnb_ref,
         h_ref, q_ref, kx_ref, vt_ref, km_ref, ysg_ref, ycv_ref, ext_ref) = refs
    else:
        (x_ref, w_ref, wvt_ref, cos_ref, sin_ref, sglng_ref, sglnb_ref, sgw_ref, sgb_ref,
         cw_ref, cb_ref, clng_ref, clnb_ref,
         q_ref, kx_ref, vt_ref, km_ref, ysg_ref, ycv_ref, ext_ref) = refs
    i = pl.program_id(0)
    h = x_ref[...]
    if apply_ln:
        h = _layer_norm(h, lng_ref[...], lnb_ref[...])
        h_ref[...] = h
    hb = h.astype(BF16)
    p = _dot(hb, w_ref[...])
    vt_ref[:, 0:HEAD_DIM, :] = _dot_nt(wvt_ref[...], hb).reshape(ATT_HEADS, HEAD_DIM, ROW_TILE).astype(BF16)
    vt_ref[:, HEAD_DIM:, :] = jnp.ones((ATT_HEADS, VT_ROWS - HEAD_DIM, ROW_TILE), BF16)

    lane = lax.broadcasted_iota(jnp.int32, (ROW_TILE, LANES), 1)
    first_half = (lane % HEAD_DIM) < (HEAD_DIM // 2)
    low_head = lane < HEAD_DIM
    cos = cos_ref[...]
    sin = sin_ref[...]

    def rot(x):
        swapped = jnp.where(first_half, pltpu.roll(x, LANES - HEAD_DIM // 2, 1), pltpu.roll(x, HEAD_DIM // 2, 1))
        return x * cos + swapped * sin

    scale = HEAD_DIM ** -0.5 * LOG2_E
    for j in range(ATT_WIDTH // LANES):
        sl = slice(j * LANES, (j + 1) * LANES)
        q_ref[:, sl] = (rot(p[:, sl]) * scale).astype(BF16)
        kr = rot(p[:, K_OFF + j * LANES:K_OFF + (j + 1) * LANES])
        km_ref[0, :, sl] = jnp.broadcast_to(jnp.mean(kr, axis=0, keepdims=True), (SUBLANES, LANES))
        kx_ref[2 * j] = jnp.where(low_head, kr, (lane - HEAD_DIM == i).astype(F32)).astype(BF16)
        kx_ref[2 * j + 1] = jnp.where(low_head, (lane == i).astype(F32), kr).astype(BF16)

    u = jax.nn.gelu(p[:, P_SGU:P_SGV])
    vv = _layer_norm(jax.nn.gelu(p[:, P_SGV:P_CA]), sglng_ref[...], sglnb_ref[...]).astype(BF16)
    r_i = lax.broadcasted_iota(jnp.int32, (SG_CHUNK, SG_CHUNK), 0)
    c_i = lax.broadcasted_iota(jnp.int32, (SG_CHUNK, SG_CHUNK), 1)
    lane_sg = lax.broadcasted_iota(jnp.int32, (SG_CHUNK, SG_WIDTH), 1)
    for c in range(ROW_TILE // SG_CHUNK):
        rows = slice(c * SG_CHUNK, (c + 1) * SG_CHUNK)
        mixed = sgb_ref[...]
        for g in range(SG_GROUPS):
            wg = jnp.where(c_i <= r_i, sgw_ref[g], 0.0).astype(BF16)
            full = _dot(wg, vv[rows])
            mixed = mixed + jnp.where(lane_sg // SG_GROUP_DIM == g, full, 0.0)
        ysg_ref[rows, :] = (u[rows] * mixed).astype(BF16)

    @pl.when(i == 0)
    def _():
        ext_ref[0:CONV_HALO, :] = jnp.zeros((CONV_HALO, CONV_WIDTH), F32)

    ext_ref[CONV_HALO:, :] = p[:, P_CA:P_CG] * jax.nn.sigmoid(p[:, P_CG:P_COLS])
    acc = jnp.zeros((ROW_TILE, CONV_WIDTH), F32) + cb_ref[...]
    first = CONV_HALO - (CONV_TAPS - 1)
    ext = ext_ref[...]
    n_ext = CONV_HALO + ROW_TILE
    for shift in range(SUBLANES):
        shifted = ext if shift == 0 else pltpu.roll(ext, n_ext - shift, 0)
        for o in range(first + (shift - first) % SUBLANES, first + CONV_TAPS, SUBLANES):
            acc = acc + cw_ref[o - first:o - first + 1, :] * shifted[o - shift:o - shift + ROW_TILE]
    ext_ref[0:CONV_HALO, :] = ext_ref[ROW_TILE:ROW_TILE + CONV_HALO, :]
    hc = _layer_norm(acc, clng_ref[...], clnb_ref[...])
    ycv_ref[...] = (hc * jax.nn.sigmoid(hc)).astype(BF16)


def _front(apply_ln, x, ln_g, ln_b, w_p, w_vt, cos, sin, sg_ln_g, sg_ln_b, sg_w, sg_bias, conv_w, conv_b,
           conv_ln_g, conv_ln_b):
    n_tiles = SEQ // ROW_TILE
    row = lambda w: pl.BlockSpec((ROW_TILE, w), lambda i: (i, 0))
    full = lambda shape: pl.BlockSpec(shape, lambda i: (0,) * len(shape))
    in_specs = [row(D_MODEL)]
    args = [x]
    if apply_ln:
        in_specs += [full((1, D_MODEL)), full((1, D_MODEL))]
        args += [ln_g, ln_b]
    in_specs += [full((D_MODEL, P_COLS)), full((ATT_WIDTH, D_MODEL)), row(LANES), row(LANES),
                 full((1, SG_WIDTH)), full((1, SG_WIDTH)),
                 full((SG_GROUPS, SG_CHUNK, SG_CHUNK)), full((SG_CHUNK, SG_WIDTH)),
                 full((CONV_TAPS, CONV_WIDTH)), full((1, CONV_WIDTH)), full((1, CONV_WIDTH)),
                 full((1, CONV_WIDTH))]
    args += [w_p, w_vt, cos, sin, sg_ln_g, sg_ln_b, sg_w, sg_bias, conv_w, conv_b, conv_ln_g, conv_ln_b]
    out_shape = [jax.ShapeDtypeStruct((SEQ, ATT_WIDTH), BF16),
                 jax.ShapeDtypeStruct((ATT_HEADS, SEQ, LANES), BF16),
                 jax.ShapeDtypeStruct((ATT_HEADS, VT_ROWS, SEQ), BF16),
                 jax.ShapeDtypeStruct((N_BLOCKS, SUBLANES, ATT_WIDTH), F32),
                 jax.ShapeDtypeStruct((SEQ, SG_WIDTH), BF16),
                 jax.ShapeDtypeStruct((SEQ, CONV_WIDTH), BF16)]
    out_specs = [row(ATT_WIDTH), pl.BlockSpec((ATT_HEADS, ROW_TILE, LANES), lambda i: (0, i, 0)),
                 pl.BlockSpec((ATT_HEADS, VT_ROWS, ROW_TILE), lambda i: (0, 0, i)),
                 pl.BlockSpec((1, SUBLANES, ATT_WIDTH), lambda i: (i, 0, 0)),
                 row(SG_WIDTH), row(CONV_WIDTH)]
    if apply_ln:
        out_shape = [jax.ShapeDtypeStruct((SEQ, D_MODEL), F32)] + out_shape
        out_specs = [row(D_MODEL)] + out_specs
    return pl.pallas_call(
        functools.partial(_front_kernel, apply_ln),
        grid=(n_tiles,),
        in_specs=in_specs,
        out_specs=out_specs,
        out_shape=out_shape,
        scratch_shapes=[pltpu.VMEM((CONV_HALO + ROW_TILE, CONV_WIDTH), F32)],
        compiler_params=pltpu.CompilerParams(dimension_semantics=("arbitrary",), vmem_limit_bytes=VMEM_LIMIT),
        name="front_ln" if apply_ln else "front",
    )(*args)


def _attn_kernel(q_ref, kx_ref, vt_ref, km_ref, o_ref, s0_ref, s1_ref, mx0_ref, mx1_ref):
    first_blk = pl.program_id(1) * ATT_Q_BLOCKS
    q = q_ref[...]

    km = km_ref[...]
    km_lane = lax.broadcasted_iota(jnp.int32, (N_BLOCKS, LANES), 1)
    km_rows = jnp.concatenate([jnp.where(km_lane >= HEAD_DIM, km, 0.0), jnp.where(km_lane < HEAD_DIM, km, 0.0)],
                              axis=0)
    km_hi, km_lo = _split_bf16(km_rows)
    eye = (lax.broadcasted_iota(jnp.int32, (LANES, LANES), 0)
           == lax.broadcasted_iota(jnp.int32, (LANES, LANES), 1)).astype(F32).astype(BF16)
    res = _dot_nt(jnp.concatenate([km_hi, km_lo, eye], axis=0), q)
    q_t = res[2 * LANES:].astype(BF16)
    gate = (res[0:LANES] + res[LANES:2 * LANES]).reshape(2, N_BLOCKS, ATT_Q_TILE)
    blk = lax.broadcasted_iota(jnp.int32, (2, N_BLOCKS, ATT_Q_TILE), 1)
    own = first_blk + lax.broadcasted_iota(jnp.int32, (2, N_BLOCKS, ATT_Q_TILE), 2) // MOBA_BLOCK
    gate = jnp.where(blk < own, gate, NEG)

    sel = jnp.zeros((2, N_BLOCKS, ATT_Q_TILE), jnp.bool_)
    blk_f = blk.astype(F32)
    for r in range(MOBA_TOPK):
        best = jnp.max(gate, axis=1, keepdims=True)
        first = jnp.min(jnp.where(gate == best, blk_f, float(N_BLOCKS)), axis=1, keepdims=True)
        pick = blk_f == first
        sel = sel | (pick & (own > r))
        gate = jnp.where(pick, -jnp.inf, gate)
    bias_t = jnp.where(sel, 0.0, NEG).astype(BF16).reshape(LANES, ATT_Q_TILE)
    low = lax.broadcasted_iota(jnp.int32, (LANES, ATT_Q_TILE), 0) < HEAD_DIM
    zero = jnp.zeros_like(q_t)
    qx_past = (jnp.where(low, q_t, bias_t), jnp.where(low, bias_t, q_t))
    qx_own = (jnp.where(low, q_t, zero), jnp.where(low, zero, q_t))

    key_i = lax.broadcasted_iota(jnp.int32, (ATT_Q_TILE, ATT_Q_TILE), 0)
    qry_i = lax.broadcasted_iota(jnp.int32, (ATT_Q_TILE, ATT_Q_TILE), 1)
    own_mask = (key_i <= qry_i) & (key_i // MOBA_BLOCK == qry_i // MOBA_BLOCK)
    start_own = pl.multiple_of(first_blk * MOBA_BLOCK, ATT_Q_TILE)

    state = []
    for par in range(2):
        s = _dot(kx_ref[par, pl.ds(start_own, ATT_Q_TILE), :], qx_own[par])
        s = jnp.where(own_mask, s, NEG)
        m = jnp.max(s, axis=0, keepdims=True)
        p = jnp.exp2(s - m)
        acc = _dot(vt_ref[par, :, pl.ds(start_own, ATT_Q_TILE)], p.astype(BF16))
        state += [m, acc]

    n_groups = lax.shift_right_logical(first_blk + (ATT_Q_BLOCKS - 1) + (ATT_GROUP - 1), ATT_GROUP_SHIFT)
    last_group = N_BLOCKS // ATT_GROUP - 1

    def scores(g, s_ref, mx_ref):
        start = pl.multiple_of(g * ATT_GROUP_KEYS, ATT_GROUP_KEYS)
        for par in range(2):
            s = _dot(kx_ref[par, pl.ds(start, ATT_GROUP_KEYS), :], qx_past[par])
            s_ref[par] = s
            mx_ref[par] = jnp.max(s, axis=0, keepdims=True)

    def consume(g, s_ref, mx_ref, carry):
        start = pl.multiple_of(g * ATT_GROUP_KEYS, ATT_GROUP_KEYS)
        out = []
        for par in range(2):
            m, acc = carry[2 * par:2 * par + 2]
            m_new = jnp.maximum(m, mx_ref[par])
            a = jnp.exp2(m - m_new)
            p = jnp.exp2(s_ref[par] - m_new)
            acc = a * acc + _dot(vt_ref[par, :, pl.ds(start, ATT_GROUP_KEYS)], p.astype(BF16))
            out += [m_new, acc]
        return tuple(out)

    scores(0, s0_ref, mx0_ref)

    def body(g, carry):
        g_next = jnp.minimum(g + 1, last_group)

        def even(c):
            scores(g_next, s1_ref, mx1_ref)
            return consume(g, s0_ref, mx0_ref, c)

        def odd(c):
            scores(g_next, s0_ref, mx0_ref)
            return consume(g, s1_ref, mx1_ref, c)

        return lax.cond((g & 1) == 0, even, odd, carry)

    state = lax.fori_loop(0, n_groups, body, tuple(state))
    o_t = jnp.concatenate([acc[0:HEAD_DIM] / acc[HEAD_DIM:HEAD_DIM + 1] for acc in (state[1], state[3])],
                          axis=0)
    o_ref[...] = o_t.T.astype(BF16)


def _attn(q, kx, vt, kmean):
    n_pairs = ATT_WIDTH // LANES
    return pl.pallas_call(
        _attn_kernel,
        grid=(n_pairs, SEQ // ATT_Q_TILE),
        in_specs=[pl.BlockSpec((ATT_Q_TILE, LANES), lambda j, t: (t, j)),
                  pl.BlockSpec((2, SEQ, LANES), lambda j, t: (j, 0, 0)),
                  pl.BlockSpec((2, VT_ROWS, SEQ), lambda j, t: (j, 0, 0)),
                  pl.BlockSpec((N_BLOCKS, LANES), lambda j, t: (0, j))],
        out_specs=pl.BlockSpec((ATT_Q_TILE, LANES), lambda j, t: (t, j)),
        out_shape=jax.ShapeDtypeStruct((SEQ, ATT_WIDTH), BF16),
        scratch_shapes=([pltpu.VMEM((2, ATT_GROUP_KEYS, ATT_Q_TILE), F32)] * 2
                        + [pltpu.VMEM((2, 1, ATT_Q_TILE), F32)] * 2),
        compiler_params=pltpu.CompilerParams(dimension_semantics=("arbitrary", "arbitrary"),
                                             vmem_limit_bytes=VMEM_LIMIT),
        name="attn",
    )(q, kx, vt, kmean)


def _outproj_kernel(h_ref, ya_ref, ys_ref, yc_ref, wo_ref, g_ref, b_ref, wr_ref, rb_ref,
                    h1_ref, h1b_ref, grp_ref, rank_ref, wk_ref, cnt_ref):
    mix = (_dot(ya_ref[...], wo_ref[0:ATT_WIDTH, :])
           + _dot(ys_ref[...], wo_ref[ATT_WIDTH:ATT_WIDTH + SG_WIDTH, :])
           + _dot(yc_ref[...], wo_ref[ATT_WIDTH + SG_WIDTH:, :]))
    h1 = _layer_norm(DEEPNORM_ALPHA * h_ref[...] + mix, g_ref[...], b_ref[...])
    h1_ref[...] = h1
    h1b_ref[...] = h1.astype(BF16)

    w_hi, w_lo = _split_bf16(wr_ref[...])
    h_hi, h_lo = _split_bf16(h1)
    both = _dot_nt(jnp.concatenate([w_hi, w_lo], axis=0), h_hi)
    logits = both[0:N_EXPERTS] + both[N_EXPERTS:] + _dot_nt(w_hi, h_lo)
    scores = jax.nn.sigmoid(logits)
    biased = scores + rb_ref[...]
    sc = [scores[e:e + 1, :] for e in range(N_EXPERTS)]
    bi = [biased[e:e + 1, :] for e in range(N_EXPERTS)]

    best = None
    for g in range(N_EXPERT_GROUPS):
        a0, a1, a2, a3 = bi[4 * g:4 * g + 4]
        hi01, lo01 = jnp.maximum(a0, a1), jnp.minimum(a0, a1)
        hi23, lo23 = jnp.maximum(a2, a3), jnp.minimum(a2, a3)
        top1 = jnp.maximum(hi01, hi23)
        top2 = jnp.maximum(jnp.minimum(hi01, hi23), jnp.maximum(lo01, lo23))
        gs = top1 + top2
        if best is None:
            best, grp = gs, jnp.zeros_like(gs, dtype=jnp.int32)
        else:
            upd = gs > best
            best = jnp.where(upd, gs, best)
            grp = jnp.where(upd, g, grp)

    def in_group(vals, k):
        out = vals[k]
        for g in range(1, N_EXPERT_GROUPS):
            out = jnp.where(grp == g, vals[4 * g + k], out)
        return out

    vb = [in_group(bi, k) for k in range(EXPERTS_PER_GROUP)]
    vs = [in_group(sc, k) for k in range(EXPERTS_PER_GROUP)]
    m1, i1, s1 = vb[0], jnp.zeros_like(grp), vs[0]
    for k in range(1, EXPERTS_PER_GROUP):
        upd = vb[k] > m1
        m1 = jnp.where(upd, vb[k], m1)
        i1 = jnp.where(upd, k, i1)
        s1 = jnp.where(upd, vs[k], s1)
    m2 = jnp.full_like(m1, -jnp.inf)
    i2 = jnp.zeros_like(grp)
    s2 = jnp.zeros_like(s1)
    for k in range(EXPERTS_PER_GROUP):
        upd = (i1 != k) & (vb[k] > m2)
        m2 = jnp.where(upd, vb[k], m2)
        i2 = jnp.where(upd, k, i2)
        s2 = jnp.where(upd, vs[k], s2)
    tot = s1 + s2
    row = lax.broadcasted_iota(jnp.int32, (GATE_ROWS, ROW_TILE), 0)
    rest = (jnp.where(row % EXPERTS_PER_GROUP == i1, s1 / tot, 0.0)
            + jnp.where(row % EXPERTS_PER_GROUP == i2, s2 / tot, 0.0))
    pieces = jnp.zeros((GATE_ROWS, ROW_TILE), F32)
    for i in range(GATE_PIECES):
        piece = rest.astype(BF16).astype(F32)
        pieces = jnp.where(row // EXPERTS_PER_GROUP == i, piece, pieces)
        rest = rest - piece
    wk_ref[...] = pieces.astype(BF16)
    grp_ref[...] = grp
    k_idx = lax.broadcasted_iota(jnp.int32, (EXPERTS_PER_GROUP, ROW_TILE), 0)

    @pl.when(pl.program_id(0) == 0)
    def _():
        cnt_ref[...] = jnp.zeros_like(cnt_ref)

    member = (k_idx == grp).astype(F32)
    t_r = lax.broadcasted_iota(jnp.int32, (ROW_TILE, ROW_TILE), 0)
    t_c = lax.broadcasted_iota(jnp.int32, (ROW_TILE, ROW_TILE), 1)
    earlier = _dot(member.astype(BF16), (t_r < t_c).astype(BF16)) + cnt_ref[...]
    rank_ref[...] = jnp.sum(member * earlier, axis=0, keepdims=True).astype(jnp.int32)
    cnt_ref[...] += jnp.sum(member, axis=1, keepdims=True)


def _outproj(h, y_att, y_sg, y_cv, w_out, ln_g, ln_b, w_router_t, router_bias):
    row = lambda w: pl.BlockSpec((ROW_TILE, w), lambda i: (i, 0))
    full = lambda shape: pl.BlockSpec(shape, lambda i: (0,) * len(shape))
    lane_row = lambda n: pl.BlockSpec((n, ROW_TILE), lambda i: (0, i))
    return pl.pallas_call(
        _outproj_kernel,
        grid=(SEQ // ROW_TILE,),
        in_specs=[row(D_MODEL), row(ATT_WIDTH), row(SG_WIDTH), row(CONV_WIDTH), full((D_MODEL, D_MODEL)),
                  full((1, D_MODEL)), full((1, D_MODEL)), full((N_EXPERTS, D_MODEL)), full((N_EXPERTS, 1))],
        out_specs=[row(D_MODEL), row(D_MODEL), lane_row(1), lane_row(1), lane_row(GATE_ROWS)],
        out_shape=[jax.ShapeDtypeStruct((SEQ, D_MODEL), F32),
                   jax.ShapeDtypeStruct((SEQ, D_MODEL), BF16),
                   jax.ShapeDtypeStruct((1, SEQ), jnp.int32),
                   jax.ShapeDtypeStruct((1, SEQ), jnp.int32),
                   jax.ShapeDtypeStruct((GATE_ROWS, SEQ), BF16)],
        scratch_shapes=[pltpu.VMEM((N_EXPERT_GROUPS, ROW_TILE), F32)],
        compiler_params=pltpu.CompilerParams(dimension_semantics=("arbitrary",), vmem_limit_bytes=VMEM_LIMIT),
        name="outproj",
    )(h, y_att, y_sg, y_cv, w_out, ln_g, ln_b, w_router_t, router_bias)


PAIR_VALID, PAIR_FIRST, PAIR_LAST, PAIR_FILL = 1, 2, 4, 8


def _moe_plan(grp, rank):
    n_src = SEQ // MOE_SRC_TILE
    member = (grp[:, None] == jnp.arange(N_EXPERT_GROUPS)[None, :]).astype(jnp.int32)
    cnt = member.reshape(n_src, MOE_SRC_TILE, N_EXPERT_GROUPS).sum(axis=1).T
    start = jnp.cumsum(cnt, axis=1) - cnt
    tiles = (cnt.sum(axis=1) + MOE_TILE - 1) // MOE_TILE
    tile0 = jnp.cumsum(tiles) - tiles
    pos = tile0[grp] * MOE_TILE + rank

    lo = jnp.arange(SEQ // MOE_TILE)[None, :, None] * MOE_TILE
    overlap = (cnt[:, None, :] > 0) & (start[:, None, :] < lo + MOE_TILE) & ((start + cnt)[:, None, :] > lo)
    n_pairs = overlap.sum()
    idx = jnp.arange(MOE_MAX_PAIRS)
    keep = jnp.minimum(idx, n_pairs - 1)

    def pair_list(mask, order):
        running = jnp.cumsum(mask.reshape(-1).astype(jnp.int32))
        flat = (running[None, :] <= keep[:, None]).sum(axis=1)
        ix = jnp.unravel_index(flat, mask.shape)
        g, il, r = (ix[order.index(a)] for a in "gtr")
        tile = tile0[g] + il
        key = tile if order[0] == "g" else r
        valid = idx < n_pairs
        first = valid & ((idx == 0) | (key != jnp.roll(key, 1)))
        last = valid & ((idx == n_pairs - 1) | (key != jnp.roll(key, -1)))
        flags = PAIR_VALID * valid + PAIR_FIRST * first + PAIR_LAST * last
        if order[0] == "g":
            spare_tile = tiles.sum() + idx - n_pairs
            fill = (idx >= n_pairs) & (spare_tile < MOE_MAX_TILES)
            tile = jnp.where(idx >= n_pairs, jnp.minimum(spare_tile, MOE_MAX_TILES - 1), tile)
            flags = flags + PAIR_FILL * fill
        return tile.astype(jnp.int32), r.astype(jnp.int32), g.astype(jnp.int32), flags.astype(jnp.int32)

    return pos, pair_list(overlap, "gtr"), pair_list(overlap.transpose(2, 0, 1), "rgt")


def _moe_ffn_kernel(pt_ref, pr_ref, pg_ref, pf_ref, h_ref, pos_ref, wk_ref, wg_ref, wu_ref, wd_ref, y_ref,
                    x_ref, w_ref):
    p = pl.program_id(0)
    flags = pf_ref[p]

    @pl.when((flags & PAIR_FIRST) != 0)
    def _():
        x_ref[...] = jnp.zeros_like(x_ref)
        w_ref[...] = jnp.zeros_like(w_ref)

    @pl.when((flags & PAIR_VALID) != 0)
    def _():
        rows = pt_ref[p] * MOE_TILE + lax.broadcasted_iota(jnp.int32, (MOE_TILE, MOE_SRC_TILE), 0)
        onehot = jnp.where(pos_ref[...] == rows, 1.0, 0.0).astype(BF16)
        x_ref[...] += _dot(onehot, h_ref[...])
        w_ref[...] += _dot_nt(onehot, wk_ref[...])

    @pl.when((flags & PAIR_LAST) != 0)
    def _():
        x = x_ref[...].astype(BF16)
        w = w_ref[...]
        w = sum(w[:, i * EXPERTS_PER_GROUP:(i + 1) * EXPERTS_PER_GROUP] for i in range(GATE_PIECES))
        y = jnp.zeros((MOE_TILE, D_MODEL), F32)
        for k in range(EXPERTS_PER_GROUP):
            a = jax.nn.silu(_dot(x, wg_ref[k])) * _dot(x, wu_ref[k])
            y = y + w[:, k:k + 1] * _dot(a.astype(BF16), wd_ref[k])
        y_hi, y_lo = _split_bf16(y)
        y_ref[:, 0:D_MODEL] = y_hi
        y_ref[:, D_MODEL:] = y_lo

    @pl.when((flags & PAIR_FILL) != 0)
    def _():
        y_ref[...] = jnp.zeros_like(y_ref)


def _moe_ffn(plan, h_bf16, pos, wk, w_gate, w_up, w_down):
    tile, src, grp, flags = plan
    src_rows = lambda w: pl.BlockSpec((MOE_SRC_TILE, w), lambda p, pt, pr, pg, pf: (pr[p], 0))
    src_lanes = lambda n: pl.BlockSpec((n, MOE_SRC_TILE), lambda p, pt, pr, pg, pf: (0, pr[p]))
    experts = lambda a, b: pl.BlockSpec((EXPERTS_PER_GROUP, a, b), lambda p, pt, pr, pg, pf: (pg[p], 0, 0))
    return pl.pallas_call(
        _moe_ffn_kernel,
        grid_spec=pltpu.PrefetchScalarGridSpec(
            num_scalar_prefetch=4,
            grid=(MOE_MAX_PAIRS,),
            in_specs=[src_rows(D_MODEL), src_lanes(1), src_lanes(GATE_ROWS),
                      experts(D_MODEL, D_EXPERT), experts(D_MODEL, D_EXPERT), experts(D_EXPERT, D_MODEL)],
            out_specs=pl.BlockSpec((MOE_TILE, 2 * D_MODEL), lambda p, pt, pr, pg, pf: (pt[p], 0)),
            scratch_shapes=[pltpu.VMEM((MOE_TILE, D_MODEL), F32), pltpu.VMEM((MOE_TILE, GATE_ROWS), F32)]),
        out_shape=jax.ShapeDtypeStruct((MOE_SORTED_ROWS, 2 * D_MODEL), BF16),
        compiler_params=pltpu.CompilerParams(dimension_semantics=("arbitrary",), vmem_limit_bytes=VMEM_LIMIT),
        name="moe_ffn",
    )(tile, src, grp, flags, h_bf16, pos, wk, w_gate, w_up, w_down)


def _moe_combine_kernel(pt_ref, pr_ref, pg_ref, pf_ref, y_ref, pos_ref, h_ref, g_ref, b_ref, o_ref, acc_ref):
    p = pl.program_id(0)
    flags = pf_ref[p]

    @pl.when((flags & PAIR_FIRST) != 0)
    def _():
        acc_ref[...] = jnp.zeros_like(acc_ref)

    @pl.when((flags & PAIR_VALID) != 0)
    def _():
        cols = pt_ref[p] * MOE_TILE + lax.broadcasted_iota(jnp.int32, (MOE_SRC_TILE, MOE_TILE), 1)
        onehot = jnp.where(pos_ref[...] == cols, 1.0, 0.0).astype(BF16)
        acc_ref[...] += _dot(onehot, y_ref[:, 0:D_MODEL]) + _dot(onehot, y_ref[:, D_MODEL:])

    @pl.when((flags & PAIR_LAST) != 0)
    def _():
        o_ref[...] = _layer_norm(DEEPNORM_ALPHA * h_ref[...] + acc_ref[...], g_ref[...], b_ref[...])


def _moe_combine(plan, y_sorted, pos_col, h, ln_g, ln_b):
    tile, src, grp, flags = plan
    src_rows = lambda w: pl.BlockSpec((MOE_SRC_TILE, w), lambda p, pt, pr, pg, pf: (pr[p], 0))
    const = lambda shape: pl.BlockSpec(shape, lambda p, pt, pr, pg, pf: (0,) * len(shape))
    return pl.pallas_call(
        _moe_combine_kernel,
        grid_spec=pltpu.PrefetchScalarGridSpec(
            num_scalar_prefetch=4,
            grid=(MOE_MAX_PAIRS,),
            in_specs=[pl.BlockSpec((MOE_TILE, 2 * D_MODEL), lambda p, pt, pr, pg, pf: (pt[p], 0)),
                      src_rows(1), src_rows(D_MODEL), const((1, D_MODEL)), const((1, D_MODEL))],
            out_specs=src_rows(D_MODEL),
            scratch_shapes=[pltpu.VMEM((MOE_SRC_TILE, D_MODEL), F32)]),
        out_shape=jax.ShapeDtypeStruct((SEQ, D_MODEL), F32),
        compiler_params=pltpu.CompilerParams(dimension_semantics=("arbitrary",), vmem_limit_bytes=VMEM_LIMIT),
        name="moe_combine",
    )(tile, src, grp, flags, y_sorted, pos_col, h, ln_g, ln_b)


def _rope_tables():
    half = HEAD_DIM // 2
    inv_freq = ROPE_THETA ** (-jnp.arange(half, dtype=F32) / half)
    ang = jnp.arange(SEQ, dtype=F32)[:, None] * inv_freq[None, :]
    cos, sin = jnp.cos(ang), jnp.sin(ang)
    reps = LANES // HEAD_DIM
    return jnp.tile(jnp.concatenate([cos, cos], axis=1), (1, reps)), jnp.tile(jnp.concatenate([-sin, sin], axis=1), (1, reps))


def kernel(x, ln_in_g, ln_in_b, w_in, w_out, sg_ln_g, sg_ln_b, sg_w, sg_b, conv_w, conv_b, conv_ln_g, conv_ln_b, ln_mix_g, ln_mix_b, w_router, router_bias, w_gate, w_up, w_down, ln_ffn_g, ln_ffn_b):
    assert x.shape == (1, SEQ, D_MODEL) and w_in.shape == (DEPTH, D_MODEL, IN_COLS)
    cos, sin = _rope_tables()
    row = lambda a: a.reshape(1, -1)
    w_router_t = w_router.T
    rb = router_bias.reshape(N_EXPERTS, 1)
    h = x.reshape(SEQ, D_MODEL)
    for l in range(DEPTH):
        sg_bias = jnp.repeat(sg_b[l].T, SG_GROUP_DIM, axis=1)
        w_p = jnp.concatenate([w_in[l][:, :V_OFF], w_in[l][:, SGU_OFF:]], axis=1).astype(BF16)
        w_vt = w_in[l][:, V_OFF:SGU_OFF].T.astype(BF16)
        args = (w_p, w_vt, cos, sin, row(sg_ln_g[l]), row(sg_ln_b[l]), sg_w[l], sg_bias,
                conv_w[l], row(conv_b[l]), row(conv_ln_g[l]), row(conv_ln_b[l]))
        if l == 0:
            h, q, kx, vt, kmean, y_sg, y_cv = _front(True, h, row(ln_in_g), row(ln_in_b), *args)
        else:
            q, kx, vt, kmean, y_sg, y_cv = _front(False, h, None, None, *args)
        y_att = _attn(q, kx, vt, kmean[:, 0, :])
        h, h_bf16, grp, rank, wk = _outproj(h, y_att, y_sg, y_cv, w_out[l].astype(BF16), row(ln_mix_g[l]),
                                            row(ln_mix_b[l]), w_router_t, rb)
        pos, by_tile, by_src = _moe_plan(grp[0], rank[0])
        y_sorted = _moe_ffn(by_tile, h_bf16, pos.reshape(1, SEQ), wk, w_gate[l].astype(BF16),
                            w_up[l].astype(BF16), w_down[l].astype(BF16))
        h = _moe_combine(by_src, y_sorted, pos.reshape(SEQ, 1), h, row(ln_ffn_g[l]), row(ln_ffn_b[l]))
    return h.reshape(1, SEQ, D_MODEL)
```

```python
import functools
import math

import jax
import jax.numpy as jnp
from jax import lax
from jax.experimental import pallas as pl
from jax.experimental.pallas import tpu as pltpu

D_MODEL = 1024
SEQ = 16384
DEPTH = 2
HEAD_DIM = 64
ATT_WIDTH = 512
ATT_HEADS = 8
SG_WIDTH = 256
SG_GROUPS = 4
SG_GROUP_DIM = 64
CONV_WIDTH = 256
K_OFF = 512
V_OFF = 1024
SGU_OFF = 1536
SGV_OFF = 1792
CA_OFF = 2048
CG_OFF = 2304
IN_COLS = 2560
P_SGU = SGU_OFF - ATT_WIDTH
P_SGV = SGV_OFF - ATT_WIDTH
P_CA = CA_OFF - ATT_WIDTH
P_CG = CG_OFF - ATT_WIDTH
P_COLS = IN_COLS - ATT_WIDTH
MOBA_BLOCK = 256
MOBA_TOPK = 3
N_BLOCKS = SEQ // MOBA_BLOCK
SG_CHUNK = 128
CONV_TAPS = 31
ROPE_THETA = 10000.0
N_EXPERTS = 16
N_EXPERT_GROUPS = 4
EXPERTS_PER_GROUP = 4
D_EXPERT = 512
LN_EPS = 1e-5
NEG = -1e30
DEEPNORM_ALPHA = (2.0 * DEPTH) ** 0.25

LANES = 128
SUBLANES = 8
ROW_TILE = MOBA_BLOCK
CONV_HALO = 32
LOG2_E = math.log2(math.e)
VT_ROWS = HEAD_DIM + 16
ATT_GROUP_SHIFT = 2
ATT_GROUP = 1 << ATT_GROUP_SHIFT
ATT_Q_BLOCKS = 2
ATT_Q_TILE = ATT_Q_BLOCKS * MOBA_BLOCK
ATT_GROUP_KEYS = ATT_GROUP * MOBA_BLOCK
GATE_PIECES = 3
GATE_ROWS = 16
MOE_TILE = 256
MOE_SRC_TILE = 512
MOE_MAX_TILES = SEQ // MOE_TILE + N_EXPERT_GROUPS
MOE_MAX_PAIRS = MOE_MAX_TILES + N_EXPERT_GROUPS * (SEQ // MOE_SRC_TILE)
MOE_SORTED_ROWS = MOE_MAX_TILES * MOE_TILE
VMEM_LIMIT = 48 * 1024 * 1024

F32 = jnp.float32
BF16 = jnp.bfloat16


def _dot(a, b):
    return jnp.dot(a, b, preferred_element_type=F32)


def _dot_nt(a, b):
    return lax.dot_general(a, b, (((1,), (1,)), ((), ())), preferred_element_type=F32)


def _layer_norm(x, g, b):
    mu = jnp.mean(x, axis=-1, keepdims=True)
    xc = x - mu
    var = jnp.mean(xc * xc, axis=-1, keepdims=True)
    return xc * lax.rsqrt(var + LN_EPS) * g + b


def _split_bf16(x):
    hi = x.astype(BF16)
    lo = (x - hi.astype(F32)).astype(BF16)
    return hi, lo


def _front_kernel(apply_ln, *refs):
    if apply_ln:
        (x_ref, lng_ref, lnb_ref, w_ref, wvt_ref, cos_ref, sin_ref, sglng_ref, sglnb_ref, sgw_ref, sgb_ref,
         cw_ref, cb_ref, clng_ref, clnb_ref,
         h_ref, q_ref, kx_ref, vt_ref, km_ref, ysg_ref, ycv_ref, ext_ref) = refs
    else:
        (x_ref, w_ref, wvt_ref, cos_ref, sin_ref, sglng_ref, sglnb_ref, sgw_ref, sgb_ref,
         cw_ref, cb_ref, clng_ref, clnb_ref,
         q_ref, kx_ref, vt_ref, km_ref, ysg_ref, ycv_ref, ext_ref) = refs
    i = pl.program_id(0)
    h = x_ref[...]
    if apply_ln:
        h = _layer_norm(h, lng_ref[...], lnb_ref[...])
        h_ref[...] = h
    hb = h.astype(BF16)
    p = _dot(hb, w_ref[...])
    vt_ref[:, 0:HEAD_DIM, :] = _dot_nt(wvt_ref[...], hb).reshape(ATT_HEADS, HEAD_DIM, ROW_TILE).astype(BF16)
    vt_ref[:, HEAD_DIM:, :] = jnp.ones((ATT_HEADS, VT_ROWS - HEAD_DIM, ROW_TILE), BF16)

    lane = lax.broadcasted_iota(jnp.int32, (ROW_TILE, LANES), 1)
    first_half = (lane % HEAD_DIM) < (HEAD_DIM // 2)
    low_head = lane < HEAD_DIM
    cos = cos_ref[...]
    sin = sin_ref[...]

    def rot(x):
        swapped = jnp.where(first_half, pltpu.roll(x, LANES - HEAD_DIM // 2, 1), pltpu.roll(x, HEAD_DIM // 2, 1))
        return x * cos + swapped * sin

    scale = HEAD_DIM ** -0.5 * LOG2_E
    for j in range(ATT_WIDTH // LANES):
        sl = slice(j * LANES, (j + 1) * LANES)
        q_ref[:, sl] = (rot(p[:, sl]) * scale).astype(BF16)
        kr = rot(p[:, K_OFF + j * LANES:K_OFF + (j + 1) * LANES])
        km_ref[0, :, sl] = jnp.broadcast_to(jnp.mean(kr, axis=0, keepdims=True), (SUBLANES, LANES))
        kx_ref[2 * j] = jnp.where(low_head, kr, (lane - HEAD_DIM == i).astype(F32)).astype(BF16)
        kx_ref[2 * j + 1] = jnp.where(low_head, (lane == i).astype(F32), kr).astype(BF16)

    u = jax.nn.gelu(p[:, P_SGU:P_SGV])
    vv = _layer_norm(jax.nn.gelu(p[:, P_SGV:P_CA]), sglng_ref[...], sglnb_ref[...]).astype(BF16)
    r_i = lax.broadcasted_iota(jnp.int32, (SG_CHUNK, SG_CHUNK), 0)
    c_i = lax.broadcasted_iota(jnp.int32, (SG_CHUNK, SG_CHUNK), 1)
    lane_sg = lax.broadcasted_iota(jnp.int32, (SG_CHUNK, SG_WIDTH), 1)
    for c in range(ROW_TILE // SG_CHUNK):
        rows = slice(c * SG_CHUNK, (c + 1) * SG_CHUNK)
        mixed = sgb_ref[...]
        for g in range(SG_GROUPS):
            wg = jnp.where(c_i <= r_i, sgw_ref[g], 0.0).astype(BF16)
            full = _dot(wg, vv[rows])
            mixed = mixed + jnp.where(lane_sg // SG_GROUP_DIM == g, full, 0.0)
        ysg_ref[rows, :] = (u[rows] * mixed).astype(BF16)

    @pl.when(i == 0)
    def _():
        ext_ref[0:CONV_HALO, :] = jnp.zeros((CONV_HALO, CONV_WIDTH), F32)

    ext_ref[CONV_HALO:, :] = p[:, P_CA:P_CG] * jax.nn.sigmoid(p[:, P_CG:P_COLS])
    acc = jnp.zeros((ROW_TILE, CONV_WIDTH), F32) + cb_ref[...]
    first = CONV_HALO - (CONV_TAPS - 1)
    ext = ext_ref[...]
    n_ext = CONV_HALO + ROW_TILE
    for shift in range(SUBLANES):
        shifted = ext if shift == 0 else pltpu.roll(ext, n_ext - shift, 0)
        for o in range(first + (shift - first) % SUBLANES, first + CONV_TAPS, SUBLANES):
            acc = acc + cw_ref[o - first:o - first + 1, :] * shifted[o - shift:o - shift + ROW_TILE]
    ext_ref[0:CONV_HALO, :] = ext_ref[ROW_TILE:ROW_TILE + CONV_HALO, :]
    hc = _layer_norm(acc, clng_ref[...], clnb_ref[...])
    ycv_ref[...] = (hc * jax.nn.sigmoid(hc)).astype(BF16)


def _front(apply_ln, x, ln_g, ln_b, w_p, w_vt, cos, sin, sg_ln_g, sg_ln_b, sg_w, sg_bias, conv_w, conv_b,
           conv_ln_g, conv_ln_b):
    n_tiles = SEQ // ROW_TILE
    row = lambda w: pl.BlockSpec((ROW_TILE, w), lambda i: (i, 0))
    full = lambda shape: pl.BlockSpec(shape, lambda i: (0,) * len(shape))
    in_specs = [row(D_MODEL)]
    args = [x]
    if apply_ln:
        in_specs += [full((1, D_MODEL)), full((1, D_MODEL))]
        args += [ln_g, ln_b]
    in_specs += [full((D_MODEL, P_COLS)), full((ATT_WIDTH, D_MODEL)), row(LANES), row(LANES),
                 full((1, SG_WIDTH)), full((1, SG_WIDTH)),
                 full((SG_GROUPS, SG_CHUNK, SG_CHUNK)), full((SG_CHUNK, SG_WIDTH)),
                 full((CONV_TAPS, CONV_WIDTH)), full((1, CONV_WIDTH)), full((1, CONV_WIDTH)),
                 full((1, CONV_WIDTH))]
    args += [w_p, w_vt, cos, sin, sg_ln_g, sg_ln_b, sg_w, sg_bias, conv_w, conv_b, conv_ln_g, conv_ln_b]
    out_shape = [jax.ShapeDtypeStruct((SEQ, ATT_WIDTH), BF16),
                 jax.ShapeDtypeStruct((ATT_HEADS, SEQ, LANES), BF16),
                 jax.ShapeDtypeStruct((ATT_HEADS, VT_ROWS, SEQ), BF16),
                 jax.ShapeDtypeStruct((N_BLOCKS, SUBLANES, ATT_WIDTH), F32),
                 jax.ShapeDtypeStruct((SEQ, SG_WIDTH), BF16),
                 jax.ShapeDtypeStruct((SEQ, CONV_WIDTH), BF16)]
    out_specs = [row(ATT_WIDTH), pl.BlockSpec((ATT_HEADS, ROW_TILE, LANES), lambda i: (0, i, 0)),
                 pl.BlockSpec((ATT_HEADS, VT_ROWS, ROW_TILE), lambda i: (0, 0, i)),
                 pl.BlockSpec((1, SUBLANES, ATT_WIDTH), lambda i: (i, 0, 0)),
                 row(SG_WIDTH), row(CONV_WIDTH)]
    if apply_ln:
        out_shape = [jax.ShapeDtypeStruct((SEQ, D_MODEL), F32)] + out_shape
        out_specs = [row(D_MODEL)] + out_specs
    return pl.pallas_call(
        functools.partial(_front_kernel, apply_ln),
        grid=(n_tiles,),
        in_specs=in_specs,
        out_specs=out_specs,
        out_shape=out_shape,
        scratch_shapes=[pltpu.VMEM((CONV_HALO + ROW_TILE, CONV_WIDTH), F32)],
        compiler_params=pltpu.CompilerParams(dimension_semantics=("arbitrary",), vmem_limit_bytes=VMEM_LIMIT),
        name="front_ln" if apply_ln else "front",
    )(*args)


def _attn_kernel(q_ref, kx_ref, vt_ref, km_ref, o_ref, s0_ref, s1_ref, mx0_ref, mx1_ref):
    first_blk = pl.program_id(1) * ATT_Q_BLOCKS
    q = q_ref[...]

    km = km_ref[...]
    km_lane = lax.broadcasted_iota(jnp.int32, (N_BLOCKS, LANES), 1)
    km_rows = jnp.concatenate([jnp.where(km_lane >= HEAD_DIM, km, 0.0), jnp.where(km_lane < HEAD_DIM, km, 0.0)],
                              axis=0)
    km_hi, km_lo = _split_bf16(km_rows)
    eye = (lax.broadcasted_iota(jnp.int32, (LANES, LANES), 0)
           == lax.broadcasted_iota(jnp.int32, (LANES, LANES), 1)).astype(F32).astype(BF16)
    res = _dot_nt(jnp.concatenate([km_hi, km_lo, eye], axis=0), q)
    q_t = res[2 * LANES:].astype(BF16)
    gate = (res[0:LANES] + res[LANES:2 * LANES]).reshape(2, N_BLOCKS, ATT_Q_TILE)
    blk = lax.broadcasted_iota(jnp.int32, (2, N_BLOCKS, ATT_Q_TILE), 1)
    own = first_blk + lax.broadcasted_iota(jnp.int32, (2, N_BLOCKS, ATT_Q_TILE), 2) // MOBA_BLOCK
    gate = jnp.where(blk < own, gate, NEG)

    sel = jnp.zeros((2, N_BLOCKS, ATT_Q_TILE), jnp.bool_)
    blk_f = blk.astype(F32)
    for r in range(MOBA_TOPK):
        best = jnp.max(gate, axis=1, keepdims=True)
        first = jnp.min(jnp.where(gate == best, blk_f, float(N_BLOCKS)), axis=1, keepdims=True)
        pick = blk_f == first
        sel = sel | (pick & (own > r))
        gate = jnp.where(pick, -jnp.inf, gate)
    bias_t = jnp.where(sel, 0.0, NEG).astype(BF16).reshape(LANES, ATT_Q_TILE)
    low = lax.broadcasted_iota(jnp.int32, (LANES, ATT_Q_TILE), 0) < HEAD_DIM
    zero = jnp.zeros_like(q_t)
    qx_past = (jnp.where(low, q_t, bias_t), jnp.where(low, bias_t, q_t))
    qx_own = (jnp.where(low, q_t, zero), jnp.where(low, zero, q_t))

    key_i = lax.broadcasted_iota(jnp.int32, (ATT_Q_TILE, ATT_Q_TILE), 0)
    qry_i = lax.broadcasted_iota(jnp.int32, (ATT_Q_TILE, ATT_Q_TILE), 1)
    own_mask = (key_i <= qry_i) & (key_i // MOBA_BLOCK == qry_i // MOBA_BLOCK)
    start_own = pl.multiple_of(first_blk * MOBA_BLOCK, ATT_Q_TILE)

    state = []
    for par in range(2):
        s = _dot(kx_ref[par, pl.ds(start_own, ATT_Q_TILE), :], qx_own[par])
        s = jnp.where(own_mask, s, NEG)
        m = jnp.max(s, axis=0, keepdims=True)
        p = jnp.exp2(s - m)
        acc = _dot(vt_ref[par, :, pl.ds(start_own, ATT_Q_TILE)], p.astype(BF16))
        state += [m, acc]

    n_groups = lax.shift_right_logical(first_blk + (ATT_Q_BLOCKS - 1) + (ATT_GROUP - 1), ATT_GROUP_SHIFT)
    last_group = N_BLOCKS // ATT_GROUP - 1

    def scores(g, s_ref, mx_ref):
        start = pl.multiple_of(g * ATT_GROUP_KEYS, ATT_GROUP_KEYS)
        for par in range(2):
            s = _dot(kx_ref[par, pl.ds(start, ATT_GROUP_KEYS), :], qx_past[par])
            s_ref[par] = s
            mx_ref[par] = jnp.max(s, axis=0, keepdims=True)

    def consume(g, s_ref, mx_ref, carry):
        start = pl.multiple_of(g * ATT_GROUP_KEYS, ATT_GROUP_KEYS)
        out = []
        for par in range(2):
            m, acc = carry[2 * par:2 * par + 2]
            m_new = jnp.maximum(m, mx_ref[par])
            a = jnp.exp2(m - m_new)
            p = jnp.exp2(s_ref[par] - m_new)
            acc = a * acc + _dot(vt_ref[par, :, pl.ds(start, ATT_GROUP_KEYS)], p.astype(BF16))
            out += [m_new, acc]
        return tuple(out)

    scores(0, s0_ref, mx0_ref)

    def body(g, carry):
        g_next = jnp.minimum(g + 1, last_group)

        def even(c):
            scores(g_next, s1_ref, mx1_ref)
            return consume(g, s0_ref, mx0_ref, c)

        def odd(c):
            scores(g_next, s0_ref, mx0_ref)
            return consume(g, s1_ref, mx1_ref, c)

        return lax.cond((g & 1) == 0, even, odd, carry)

    state = lax.fori_loop(0, n_groups, body, tuple(state))
    o_t = jnp.concatenate([acc[0:HEAD_DIM] / acc[HEAD_DIM:HEAD_DIM + 1] for acc in (state[1], state[3])],
                          axis=0)
    o_ref[...] = o_t.T.astype(BF16)


def _attn(q, kx, vt, kmean):
    n_pairs = ATT_WIDTH // LANES
    return pl.pallas_call(
        _attn_kernel,
        grid=(n_pairs, SEQ // ATT_Q_TILE),
        in_specs=[pl.BlockSpec((ATT_Q_TILE, LANES), lambda j, t: (t, j)),
                  pl.BlockSpec((2, SEQ, LANES), lambda j, t: (j, 0, 0)),
                  pl.BlockSpec((2, VT_ROWS, SEQ), lambda j, t: (j, 0, 0)),
                  pl.BlockSpec((N_BLOCKS, LANES), lambda j, t: (0, j))],
        out_specs=pl.BlockSpec((ATT_Q_TILE, LANES), lambda j, t: (t, j)),
        out_shape=jax.ShapeDtypeStruct((SEQ, ATT_WIDTH), BF16),
        scratch_shapes=([pltpu.VMEM((2, ATT_GROUP_KEYS, ATT_Q_TILE), F32)] * 2
                        + [pltpu.VMEM((2, 1, ATT_Q_TILE), F32)] * 2),
        compiler_params=pltpu.CompilerParams(dimension_semantics=("arbitrary", "arbitrary"),
                                             vmem_limit_bytes=VMEM_LIMIT),
        name="attn",
    )(q, kx, vt, kmean)


def _outproj_kernel(h_ref, ya_ref, ys_ref, yc_ref, wo_ref, g_ref, b_ref, wr_ref, rb_ref,
                    h1_ref, h1b_ref, grp_ref, rank_ref, wk_ref, cnt_ref):
    mix = (_dot(ya_ref[...], wo_ref[0:ATT_WIDTH, :])
           + _dot(ys_ref[...], wo_ref[ATT_WIDTH:ATT_WIDTH + SG_WIDTH, :])
           + _dot(yc_ref[...], wo_ref[ATT_WIDTH + SG_WIDTH:, :]))
    h1 = _layer_norm(DEEPNORM_ALPHA * h_ref[...] + mix, g_ref[...], b_ref[...])
    h1_ref[...] = h1
    h1b_ref[...] = h1.astype(BF16)

    w_hi, w_lo = _split_bf16(wr_ref[...])
    h_hi, h_lo = _split_bf16(h1)
    both = _dot_nt(jnp.concatenate([w_hi, w_lo], axis=0), h_hi)
    logits = both[0:N_EXPERTS] + both[N_EXPERTS:] + _dot_nt(w_hi, h_lo)
    scores = jax.nn.sigmoid(logits)
    biased = scores + rb_ref[...]
    sc = [scores[e:e + 1, :] for e in range(N_EXPERTS)]
    bi = [biased[e:e + 1, :] for e in range(N_EXPERTS)]

    best = None
    for g in range(N_EXPERT_GROUPS):
        a0, a1, a2, a3 = bi[4 * g:4 * g + 4]
        hi01, lo01 = jnp.maximum(a0, a1), jnp.minimum(a0, a1)
        hi23, lo23 = jnp.maximum(a2, a3), jnp.minimum(a2, a3)
        top1 = jnp.maximum(hi01, hi23)
        top2 = jnp.maximum(jnp.minimum(hi01, hi23), jnp.maximum(lo01, lo23))
        gs = top1 + top2
        if best is None:
            best, grp = gs, jnp.zeros_like(gs, dtype=jnp.int32)
        else:
            upd = gs > best
            best = jnp.where(upd, gs, best)
            grp = jnp.where(upd, g, grp)

    def in_group(vals, k):
        out = vals[k]
        for g in range(1, N_EXPERT_GROUPS):
            out = jnp.where(grp == g, vals[4 * g + k], out)
        return out

    vb = [in_group(bi, k) for k in range(EXPERTS_PER_GROUP)]
    vs = [in_group(sc, k) for k in range(EXPERTS_PER_GROUP)]
    m1, i1, s1 = vb[0], jnp.zeros_like(grp), vs[0]
    for k in range(1, EXPERTS_PER_GROUP):
        upd = vb[k] > m1
        m1 = jnp.where(upd, vb[k], m1)
        i1 = jnp.where(upd, k, i1)
        s1 = jnp.where(upd, vs[k], s1)
    m2 = jnp.full_like(m1, -jnp.inf)
    i2 = jnp.zeros_like(grp)
    s2 = jnp.zeros_like(s1)
    for k in range(EXPERTS_PER_GROUP):
        upd = (i1 != k) & (vb[k] > m2)
        m2 = jnp.where(upd, vb[k], m2)
        i2 = jnp.where(upd, k, i2)
        s2 = jnp.where(upd, vs[k], s2)
    tot = s1 + s2
    row = lax.broadcasted_iota(jnp.int32, (GATE_ROWS, ROW_TILE), 0)
    rest = (jnp.where(row % EXPERTS_PER_GROUP == i1, s1 / tot, 0.0)
            + jnp.where(row % EXPERTS_PER_GROUP == i2, s2 / tot, 0.0))
    pieces = jnp.zeros((GATE_ROWS, ROW_TILE), F32)
    for i in range(GATE_PIECES):
        piece = rest.astype(BF16).astype(F32)
        pieces = jnp.where(row // EXPERTS_PER_GROUP == i, piece, pieces)
        rest = rest - piece
    wk_ref[...] = pieces.astype(BF16)
    grp_ref[...] = grp
    k_idx = lax.broadcasted_iota(jnp.int32, (EXPERTS_PER_GROUP, ROW_TILE), 0)

    @pl.when(pl.program_id(0) == 0)
    def _():
        cnt_ref[...] = jnp.zeros_like(cnt_ref)

    member = (k_idx == grp).astype(F32)
    t_r = lax.broadcasted_iota(jnp.int32, (ROW_TILE, ROW_TILE), 0)
    t_c = lax.broadcasted_iota(jnp.int32, (ROW_TILE, ROW_TILE), 1)
    earlier = _dot(member.astype(BF16), (t_r < t_c).astype(BF16)) + cnt_ref[...]
    rank_ref[...] = jnp.sum(member * earlier, axis=0, keepdims=True).astype(jnp.int32)
    cnt_ref[...] += jnp.sum(member, axis=1, keepdims=True)


def _outproj(h, y_att, y_sg, y_cv, w_out, ln_g, ln_b, w_router_t, router_bias):
    row = lambda w: pl.BlockSpec((ROW_TILE, w), lambda i: (i, 0))
    full = lambda shape: pl.BlockSpec(shape, lambda i: (0,) * len(shape))
    lane_row = lambda n: pl.BlockSpec((n, ROW_TILE), lambda i: (0, i))
    return pl.pallas_call(
        _outproj_kernel,
        grid=(SEQ // ROW_TILE,),
        in_specs=[row(D_MODEL), row(ATT_WIDTH), row(SG_WIDTH), row(CONV_WIDTH), full((D_MODEL, D_MODEL)),
                  full((1, D_MODEL)), full((1, D_MODEL)), full((N_EXPERTS, D_MODEL)), full((N_EXPERTS, 1))],
        out_specs=[row(D_MODEL), row(D_MODEL), lane_row(1), lane_row(1), lane_row(GATE_ROWS)],
        out_shape=[jax.ShapeDtypeStruct((SEQ, D_MODEL), F32),
                   jax.ShapeDtypeStruct((SEQ, D_MODEL), BF16),
                   jax.ShapeDtypeStruct((1, SEQ), jnp.int32),
                   jax.ShapeDtypeStruct((1, SEQ), jnp.int32),
                   jax.ShapeDtypeStruct((GATE_ROWS, SEQ), BF16)],
        scratch_shapes=[pltpu.VMEM((N_EXPERT_GROUPS, ROW_TILE), F32)],
        compiler_params=pltpu.CompilerParams(dimension_semantics=("arbitrary",), vmem_limit_bytes=VMEM_LIMIT),
        name="outproj",
    )(h, y_att, y_sg, y_cv, w_out, ln_g, ln_b, w_router_t, router_bias)


PAIR_VALID, PAIR_FIRST, PAIR_LAST, PAIR_FILL = 1, 2, 4, 8


def _moe_plan(grp, rank):
    n_src = SEQ // MOE_SRC_TILE
    member = (grp[:, None] == jnp.arange(N_EXPERT_GROUPS)[None, :]).astype(jnp.int32)
    cnt = member.reshape(n_src, MOE_SRC_TILE, N_EXPERT_GROUPS).sum(axis=1).T
    start = jnp.cumsum(cnt, axis=1) - cnt
    tiles = (cnt.sum(axis=1) + MOE_TILE - 1) // MOE_TILE
    tile0 = jnp.cumsum(tiles) - tiles
    pos = tile0[grp] * MOE_TILE + rank

    lo = jnp.arange(SEQ // MOE_TILE)[None, :, None] * MOE_TILE
    overlap = (cnt[:, None, :] > 0) & (start[:, None, :] < lo + MOE_TILE) & ((start + cnt)[:, None, :] > lo)
    n_pairs = overlap.sum()
    idx = jnp.arange(MOE_MAX_PAIRS)
    keep = jnp.minimum(idx, n_pairs - 1)

    def pair_list(mask, order):
        running = jnp.cumsum(mask.reshape(-1).astype(jnp.int32))
        flat = (running[None, :] <= keep[:, None]).sum(axis=1)
        ix = jnp.unravel_index(flat, mask.shape)
        g, il, r = (ix[order.index(a)] for a in "gtr")
        tile = tile0[g] + il
        key = tile if order[0] == "g" else r
        valid = idx < n_pairs
        first = valid & ((idx == 0) | (key != jnp.roll(key, 1)))
        last = valid & ((idx == n_pairs - 1) | (key != jnp.roll(key, -1)))
        flags = PAIR_VALID * valid + PAIR_FIRST * first + PAIR_LAST * last
        if order[0] == "g":
            spare_tile = tiles.sum() + idx - n_pairs
            fill = (idx >= n_pairs) & (spare_tile < MOE_MAX_TILES)
            tile = jnp.where(idx >= n_pairs, jnp.minimum(spare_tile, MOE_MAX_TILES - 1), tile)
            flags = flags + PAIR_FILL * fill
        return tile.astype(jnp.int32), r.astype(jnp.int32), g.astype(jnp.int32), flags.astype(jnp.int32)

    return pos, pair_list(overlap, "gtr"), pair_list(overlap.transpose(2, 0, 1), "rgt")


def _moe_ffn_kernel(pt_ref, pr_ref, pg_ref, pf_ref, h_ref, pos_ref, wk_ref, wg_ref, wu_ref, wd_ref, y_ref,
                    x_ref, w_ref):
    p = pl.program_id(0)
    flags = pf_ref[p]

    @pl.when((flags & PAIR_FIRST) != 0)
    def _():
        x_ref[...] = jnp.zeros_like(x_ref)
        w_ref[...] = jnp.zeros_like(w_ref)

    @pl.when((flags & PAIR_VALID) != 0)
    def _():
        rows = pt_ref[p] * MOE_TILE + lax.broadcasted_iota(jnp.int32, (MOE_TILE, MOE_SRC_TILE), 0)
        onehot = jnp.where(pos_ref[...] == rows, 1.0, 0.0).astype(BF16)
        x_ref[...] += _dot(onehot, h_ref[...])
        w_ref[...] += _dot_nt(onehot, wk_ref[...])

    @pl.when((flags & PAIR_LAST) != 0)
    def _():
        x = x_ref[...].astype(BF16)
        w = w_ref[...]
        w = sum(w[:, i * EXPERTS_PER_GROUP:(i + 1) * EXPERTS_PER_GROUP] for i in range(GATE_PIECES))
        y = jnp.zeros((MOE_TILE, D_MODEL), F32)
        for k in range(EXPERTS_PER_GROUP):
            a = jax.nn.silu(_dot(x, wg_ref[0, k])) * _dot(x, wu_ref[0, k])
            y = y + w[:, k:k + 1] * _dot(a.astype(BF16), wd_ref[0, k])
        y_hi, y_lo = _split_bf16(y)
        y_ref[:, 0:D_MODEL] = y_hi
        y_ref[:, D_MODEL:] = y_lo

    @pl.when((flags & PAIR_FILL) != 0)
    def _():
        y_ref[...] = jnp.zeros_like(y_ref)


def _moe_ffn(layer, plan, h_bf16, pos, wk, w_gate, w_up, w_down):
    tile, src, grp, flags = plan
    src_rows = lambda w: pl.BlockSpec((MOE_SRC_TILE, w), lambda p, pt, pr, pg, pf: (pr[p], 0))
    src_lanes = lambda n: pl.BlockSpec((n, MOE_SRC_TILE), lambda p, pt, pr, pg, pf: (0, pr[p]))
    experts = lambda a, b: pl.BlockSpec((1, EXPERTS_PER_GROUP, a, b),
                                        lambda p, pt, pr, pg, pf: (layer, pg[p], 0, 0))
    return pl.pallas_call(
        _moe_ffn_kernel,
        grid_spec=pltpu.PrefetchScalarGridSpec(
            num_scalar_prefetch=4,
            grid=(MOE_MAX_PAIRS,),
            in_specs=[src_rows(D_MODEL), src_lanes(1), src_lanes(GATE_ROWS),
                      experts(D_MODEL, D_EXPERT), experts(D_MODEL, D_EXPERT), experts(D_EXPERT, D_MODEL)],
            out_specs=pl.BlockSpec((MOE_TILE, 2 * D_MODEL), lambda p, pt, pr, pg, pf: (pt[p], 0)),
            scratch_shapes=[pltpu.VMEM((MOE_TILE, D_MODEL), F32), pltpu.VMEM((MOE_TILE, GATE_ROWS), F32)]),
        out_shape=jax.ShapeDtypeStruct((MOE_SORTED_ROWS, 2 * D_MODEL), BF16),
        compiler_params=pltpu.CompilerParams(dimension_semantics=("arbitrary",), vmem_limit_bytes=VMEM_LIMIT),
        name="moe_ffn",
    )(tile, src, grp, flags, h_bf16, pos, wk, w_gate, w_up, w_down)


def _moe_combine_kernel(pt_ref, pr_ref, pg_ref, pf_ref, y_ref, pos_ref, h_ref, g_ref, b_ref, o_ref, acc_ref):
    p = pl.program_id(0)
    flags = pf_ref[p]

    @pl.when((flags & PAIR_FIRST) != 0)
    def _():
        acc_ref[...] = jnp.zeros_like(acc_ref)

    @pl.when((flags & PAIR_VALID) != 0)
    def _():
        cols = pt_ref[p] * MOE_TILE + lax.broadcasted_iota(jnp.int32, (MOE_SRC_TILE, MOE_TILE), 1)
        onehot = jnp.where(pos_ref[...] == cols, 1.0, 0.0).astype(BF16)
        acc_ref[...] += _dot(onehot, y_ref[:, 0:D_MODEL]) + _dot(onehot, y_ref[:, D_MODEL:])

    @pl.when((flags & PAIR_LAST) != 0)
    def _():
        o_ref[...] = _layer_norm(DEEPNORM_ALPHA * h_ref[...] + acc_ref[...], g_ref[...], b_ref[...])


def _moe_combine(plan, y_sorted, pos_col, h, ln_g, ln_b):
    tile, src, grp, flags = plan
    src_rows = lambda w: pl.BlockSpec((MOE_SRC_TILE, w), lambda p, pt, pr, pg, pf: (pr[p], 0))
    const = lambda shape: pl.BlockSpec(shape, lambda p, pt, pr, pg, pf: (0,) * len(shape))
    return pl.pallas_call(
        _moe_combine_kernel,
        grid_spec=pltpu.PrefetchScalarGridSpec(
            num_scalar_prefetch=4,
            grid=(MOE_MAX_PAIRS,),
            in_specs=[pl.BlockSpec((MOE_TILE, 2 * D_MODEL), lambda p, pt, pr, pg, pf: (pt[p], 0)),
                      src_rows(1), src_rows(D_MODEL), const((1, D_MODEL)), const((1, D_MODEL))],
            out_specs=src_rows(D_MODEL),
            scratch_shapes=[pltpu.VMEM((MOE_SRC_TILE, D_MODEL), F32)]),
        out_shape=jax.ShapeDtypeStruct((SEQ, D_MODEL), F32),
        compiler_params=pltpu.CompilerParams(dimension_semantics=("arbitrary",), vmem_limit_bytes=VMEM_LIMIT),
        name="moe_combine",
    )(tile, src, grp, flags, y_sorted, pos_col, h, ln_g, ln_b)


def _rope_tables():
    half = HEAD_DIM // 2
    inv_freq = ROPE_THETA ** (-jnp.arange(half, dtype=F32) / half)
    ang = jnp.arange(SEQ, dtype=F32)[:, None] * inv_freq[None, :]
    cos, sin = jnp.cos(ang), jnp.sin(ang)
    reps = LANES // HEAD_DIM
    return jnp.tile(jnp.concatenate([cos, cos], axis=1), (1, reps)), jnp.tile(jnp.concatenate([-sin, sin], axis=1), (1, reps))


def kernel(x, ln_in_g, ln_in_b, w_in, w_out, sg_ln_g, sg_ln_b, sg_w, sg_b, conv_w, conv_b, conv_ln_g, conv_ln_b, ln_mix_g, ln_mix_b, w_router, router_bias, w_gate, w_up, w_down, ln_ffn_g, ln_ffn_b):
    assert x.shape == (1, SEQ, D_MODEL) and w_in.shape == (DEPTH, D_MODEL, IN_COLS)
    cos, sin = _rope_tables()
    row = lambda a: a.reshape(1, -1)
    w_router_t = w_router.T
    rb = router_bias.reshape(N_EXPERTS, 1)
    w_gate_bf16, w_up_bf16, w_down_bf16 = w_gate.astype(BF16), w_up.astype(BF16), w_down.astype(BF16)
    h = x.reshape(SEQ, D_MODEL)
    for l in range(DEPTH):
        sg_bias = jnp.repeat(sg_b[l].T, SG_GROUP_DIM, axis=1)
        w_p = jnp.concatenate([w_in[l][:, :V_OFF], w_in[l][:, SGU_OFF:]], axis=1).astype(BF16)
        w_vt = w_in[l][:, V_OFF:SGU_OFF].T.astype(BF16)
        args = (w_p, w_vt, cos, sin, row(sg_ln_g[l]), row(sg_ln_b[l]), sg_w[l], sg_bias,
                conv_w[l], row(conv_b[l]), row(conv_ln_g[l]), row(conv_ln_b[l]))
        if l == 0:
            h, q, kx, vt, kmean, y_sg, y_cv = _front(True, h, row(ln_in_g), row(ln_in_b), *args)
        else:
            q, kx, vt, kmean, y_sg, y_cv = _front(False, h, None, None, *args)
        y_att = _attn(q, kx, vt, kmean[:, 0, :])
        h, h_bf16, grp, rank, wk = _outproj(h, y_att, y_sg, y_cv, w_out[l].astype(BF16), row(ln_mix_g[l]),
                                            row(ln_mix_b[l]), w_router_t, rb)
        pos, by_tile, by_src = _moe_plan(grp[0], rank[0])
        y_sorted = _moe_ffn(l, by_tile, h_bf16, pos.reshape(1, SEQ), wk, w_gate_bf16, w_up_bf16, w_down_bf16)
        h = _moe_combine(by_src, y_sorted, pos.reshape(SEQ, 1), h, row(ln_ffn_g[l]), row(ln_ffn_b[l]))
    return h.reshape(1, SEQ, D_MODEL)
```

```python
import functools
import math

import jax
import jax.numpy as jnp
from jax import lax
from jax.experimental import pallas as pl
from jax.experimental.pallas import tpu as pltpu

D_MODEL = 1024
SEQ = 16384
DEPTH = 2
HEAD_DIM = 64
ATT_WIDTH = 512
ATT_HEADS = 8
SG_WIDTH = 256
SG_GROUPS = 4
SG_GROUP_DIM = 64
CONV_WIDTH = 256
K_OFF = 512
V_OFF = 1024
SGU_OFF = 1536
SGV_OFF = 1792
CA_OFF = 2048
CG_OFF = 2304
IN_COLS = 2560
P_SGU = SGU_OFF - ATT_WIDTH
P_SGV = SGV_OFF - ATT_WIDTH
P_CA = CA_OFF - ATT_WIDTH
P_CG = CG_OFF - ATT_WIDTH
P_COLS = IN_COLS - ATT_WIDTH
MOBA_BLOCK = 256
MOBA_TOPK = 3
N_BLOCKS = SEQ // MOBA_BLOCK
SG_CHUNK = 128
CONV_TAPS = 31
ROPE_THETA = 10000.0
N_EXPERTS = 16
N_EXPERT_GROUPS = 4
EXPERTS_PER_GROUP = 4
D_EXPERT = 512
LN_EPS = 1e-5
NEG = -1e30
DEEPNORM_ALPHA = (2.0 * DEPTH) ** 0.25

LANES = 128
SUBLANES = 8
ROW_TILE = MOBA_BLOCK
CONV_HALO = 32
LOG2_E = math.log2(math.e)
VT_ROWS = HEAD_DIM + 16
ATT_GROUP_SHIFT = 2
ATT_GROUP = 1 << ATT_GROUP_SHIFT
ATT_Q_BLOCKS = 2
ATT_Q_TILE = ATT_Q_BLOCKS * MOBA_BLOCK
ATT_GROUP_KEYS = ATT_GROUP * MOBA_BLOCK
GATE_PIECES = 3
GATE_ROWS = 16
MOE_TILE = 256
MOE_SRC_TILE = 512
MOE_MAX_TILES = SEQ // MOE_TILE + N_EXPERT_GROUPS
MOE_MAX_PAIRS = MOE_MAX_TILES + N_EXPERT_GROUPS * (SEQ // MOE_SRC_TILE)
MOE_SORTED_ROWS = MOE_MAX_TILES * MOE_TILE
VMEM_LIMIT = 48 * 1024 * 1024

F32 = jnp.float32
BF16 = jnp.bfloat16


def _dot(a, b):
    return jnp.dot(a, b, preferred_element_type=F32)


def _dot_nt(a, b):
    return lax.dot_general(a, b, (((1,), (1,)), ((), ())), preferred_element_type=F32)


def _layer_norm(x, g, b):
    mu = jnp.mean(x, axis=-1, keepdims=True)
    xc = x - mu
    var = jnp.mean(xc * xc, axis=-1, keepdims=True)
    return xc * lax.rsqrt(var + LN_EPS) * g + b


def _split_bf16(x):
    hi = x.astype(BF16)
    lo = (x - hi.astype(F32)).astype(BF16)
    return hi, lo


def _front_kernel(apply_ln, *refs):
    if apply_ln:
        (x_ref, lng_ref, lnb_ref, w_ref, wvt_ref, cos_ref, sin_ref, sglng_ref, sglnb_ref, sgw_ref, sgb_ref,
         cw_ref, cb_ref, clng_ref, clnb_ref,
         h_ref, q_ref, kx_ref, vt_ref, km_ref, ysg_ref, ycv_ref, ext_ref) = refs
    else:
        (x_ref, w_ref, wvt_ref, cos_ref, sin_ref, sglng_ref, sglnb_ref, sgw_ref, sgb_ref,
         cw_ref, cb_ref, clng_ref, clnb_ref,
         q_ref, kx_ref, vt_ref, km_ref, ysg_ref, ycv_ref, ext_ref) = refs
    i = pl.program_id(0)
    h = x_ref[...]
    if apply_ln:
        h = _layer_norm(h, lng_ref[...], lnb_ref[...])
        h_ref[...] = h
    hb = h.astype(BF16)
    p = _dot(hb, w_ref[...])
    vt_ref[:, 0:HEAD_DIM, :] = _dot_nt(wvt_ref[...], hb).reshape(ATT_HEADS, HEAD_DIM, ROW_TILE).astype(BF16)
    vt_ref[:, HEAD_DIM:, :] = jnp.ones((ATT_HEADS, VT_ROWS - HEAD_DIM, ROW_TILE), BF16)

    lane = lax.broadcasted_iota(jnp.int32, (ROW_TILE, LANES), 1)
    first_half = (lane % HEAD_DIM) < (HEAD_DIM // 2)
    low_head = lane < HEAD_DIM
    cos = cos_ref[...]
    sin = sin_ref[...]

    def rot(x):
        swapped = jnp.where(first_half, pltpu.roll(x, LANES - HEAD_DIM // 2, 1), pltpu.roll(x, HEAD_DIM // 2, 1))
        return x * cos + swapped * sin

    scale = HEAD_DIM ** -0.5 * LOG2_E
    for j in range(ATT_WIDTH // LANES):
        sl = slice(j * LANES, (j + 1) * LANES)
        q_ref[:, sl] = (rot(p[:, sl]) * scale).astype(BF16)
        kr = rot(p[:, K_OFF + j * LANES:K_OFF + (j + 1) * LANES])
        km_ref[0, :, sl] = jnp.broadcast_to(jnp.mean(kr, axis=0, keepdims=True), (SUBLANES, LANES))
        kx_ref[2 * j] = jnp.where(low_head, kr, (lane - HEAD_DIM == i).astype(F32)).astype(BF16)
        kx_ref[2 * j + 1] = jnp.where(low_head, (lane == i).astype(F32), kr).astype(BF16)

    u = jax.nn.gelu(p[:, P_SGU:P_SGV])
    vv = _layer_norm(jax.nn.gelu(p[:, P_SGV:P_CA]), sglng_ref[...], sglnb_ref[...]).astype(BF16)
    r_i = lax.broadcasted_iota(jnp.int32, (SG_CHUNK, SG_CHUNK), 0)
    c_i = lax.broadcasted_iota(jnp.int32, (SG_CHUNK, SG_CHUNK), 1)
    lane_sg = lax.broadcasted_iota(jnp.int32, (SG_CHUNK, SG_WIDTH), 1)
    for c in range(ROW_TILE // SG_CHUNK):
        rows = slice(c * SG_CHUNK, (c + 1) * SG_CHUNK)
        mixed = sgb_ref[...]
        for g in range(SG_GROUPS):
            wg = jnp.where(c_i <= r_i, sgw_ref[g], 0.0).astype(BF16)
            full = _dot(wg, vv[rows])
            mixed = mixed + jnp.where(lane_sg // SG_GROUP_DIM == g, full, 0.0)
        ysg_ref[rows, :] = (u[rows] * mixed).astype(BF16)

    @pl.when(i == 0)
    def _():
        ext_ref[0:CONV_HALO, :] = jnp.zeros((CONV_HALO, CONV_WIDTH), F32)

    ext_ref[CONV_HALO:, :] = p[:, P_CA:P_CG] * jax.nn.sigmoid(p[:, P_CG:P_COLS])
    acc = jnp.zeros((ROW_TILE, CONV_WIDTH), F32) + cb_ref[...]
    first = CONV_HALO - (CONV_TAPS - 1)
    ext = ext_ref[...]
    n_ext = CONV_HALO + ROW_TILE
    for shift in range(SUBLANES):
        shifted = ext if shift == 0 else pltpu.roll(ext, n_ext - shift, 0)
        for o in range(first + (shift - first) % SUBLANES, first + CONV_TAPS, SUBLANES):
            acc = acc + cw_ref[o - first:o - first + 1, :] * shifted[o - shift:o - shift + ROW_TILE]
    ext_ref[0:CONV_HALO, :] = ext_ref[ROW_TILE:ROW_TILE + CONV_HALO, :]
    hc = _layer_norm(acc, clng_ref[...], clnb_ref[...])
    ycv_ref[...] = (hc * jax.nn.sigmoid(hc)).astype(BF16)


def _front(apply_ln, x, ln_g, ln_b, w_p, w_vt, cos, sin, sg_ln_g, sg_ln_b, sg_w, sg_bias, conv_w, conv_b,
           conv_ln_g, conv_ln_b):
    n_tiles = SEQ // ROW_TILE
    row = lambda w: pl.BlockSpec((ROW_TILE, w), lambda i: (i, 0))
    full = lambda shape: pl.BlockSpec(shape, lambda i: (0,) * len(shape))
    in_specs = [row(D_MODEL)]
    args = [x]
    if apply_ln:
        in_specs += [full((1, D_MODEL)), full((1, D_MODEL))]
        args += [ln_g, ln_b]
    in_specs += [full((D_MODEL, P_COLS)), full((ATT_WIDTH, D_MODEL)), row(LANES), row(LANES),
                 full((1, SG_WIDTH)), full((1, SG_WIDTH)),
                 full((SG_GROUPS, SG_CHUNK, SG_CHUNK)), full((SG_CHUNK, SG_WIDTH)),
                 full((CONV_TAPS, CONV_WIDTH)), full((1, CONV_WIDTH)), full((1, CONV_WIDTH)),
                 full((1, CONV_WIDTH))]
    args += [w_p, w_vt, cos, sin, sg_ln_g, sg_ln_b, sg_w, sg_bias, conv_w, conv_b, conv_ln_g, conv_ln_b]
    out_shape = [jax.ShapeDtypeStruct((SEQ, ATT_WIDTH), BF16),
                 jax.ShapeDtypeStruct((ATT_HEADS, SEQ, LANES), BF16),
                 jax.ShapeDtypeStruct((ATT_HEADS, VT_ROWS, SEQ), BF16),
                 jax.ShapeDtypeStruct((N_BLOCKS, SUBLANES, ATT_WIDTH), F32),
                 jax.ShapeDtypeStruct((SEQ, SG_WIDTH), BF16),
                 jax.ShapeDtypeStruct((SEQ, CONV_WIDTH), BF16)]
    out_specs = [row(ATT_WIDTH), pl.BlockSpec((ATT_HEADS, ROW_TILE, LANES), lambda i: (0, i, 0)),
                 pl.BlockSpec((ATT_HEADS, VT_ROWS, ROW_TILE), lambda i: (0, 0, i)),
                 pl.BlockSpec((1, SUBLANES, ATT_WIDTH), lambda i: (i, 0, 0)),
                 row(SG_WIDTH), row(CONV_WIDTH)]
    if apply_ln:
        out_shape = [jax.ShapeDtypeStruct((SEQ, D_MODEL), F32)] + out_shape
        out_specs = [row(D_MODEL)] + out_specs
    return pl.pallas_call(
        functools.partial(_front_kernel, apply_ln),
        grid=(n_tiles,),
        in_specs=in_specs,
        out_specs=out_specs,
        out_shape=out_shape,
        scratch_shapes=[pltpu.VMEM((CONV_HALO + ROW_TILE, CONV_WIDTH), F32)],
        compiler_params=pltpu.CompilerParams(dimension_semantics=("arbitrary",), vmem_limit_bytes=VMEM_LIMIT),
        name="front_ln" if apply_ln else "front",
    )(*args)


def _attn_kernel(q_ref, kx_ref, vt_ref, km_ref, o_ref, s0_ref, s1_ref, mx0_ref, mx1_ref):
    first_blk = pl.program_id(1) * ATT_Q_BLOCKS
    q = q_ref[...]

    km = km_ref[...]
    km_lane = lax.broadcasted_iota(jnp.int32, (N_BLOCKS, LANES), 1)
    km_rows = jnp.concatenate([jnp.where(km_lane >= HEAD_DIM, km, 0.0), jnp.where(km_lane < HEAD_DIM, km, 0.0)],
                              axis=0)
    km_hi, km_lo = _split_bf16(km_rows)
    eye = (lax.broadcasted_iota(jnp.int32, (LANES, LANES), 0)
           == lax.broadcasted_iota(jnp.int32, (LANES, LANES), 1)).astype(F32).astype(BF16)
    res = _dot_nt(jnp.concatenate([km_hi, km_lo, eye], axis=0), q)
    q_t = res[2 * LANES:].astype(BF16)
    gate = (res[0:LANES] + res[LANES:2 * LANES]).reshape(2, N_BLOCKS, ATT_Q_TILE)
    blk = lax.broadcasted_iota(jnp.int32, (2, N_BLOCKS, ATT_Q_TILE), 1)
    own = first_blk + lax.broadcasted_iota(jnp.int32, (2, N_BLOCKS, ATT_Q_TILE), 2) // MOBA_BLOCK
    gate = jnp.where(blk < own, gate, NEG)

    sel = jnp.zeros((2, N_BLOCKS, ATT_Q_TILE), jnp.bool_)
    blk_f = blk.astype(F32)
    for r in range(MOBA_TOPK):
        best = jnp.max(gate, axis=1, keepdims=True)
        first = jnp.min(jnp.where(gate == best, blk_f, float(N_BLOCKS)), axis=1, keepdims=True)
        pick = blk_f == first
        sel = sel | (pick & (own > r))
        gate = jnp.where(pick, -jnp.inf, gate)
    bias_t = jnp.where(sel, 0.0, NEG).astype(BF16).reshape(LANES, ATT_Q_TILE)
    low = lax.broadcasted_iota(jnp.int32, (LANES, ATT_Q_TILE), 0) < HEAD_DIM
    zero = jnp.zeros_like(q_t)
    qx_past = (jnp.where(low, q_t, bias_t), jnp.where(low, bias_t, q_t))
    qx_own = (jnp.where(low, q_t, zero), jnp.where(low, zero, q_t))

    key_i = lax.broadcasted_iota(jnp.int32, (ATT_Q_TILE, ATT_Q_TILE), 0)
    qry_i = lax.broadcasted_iota(jnp.int32, (ATT_Q_TILE, ATT_Q_TILE), 1)
    own_mask = (key_i <= qry_i) & (key_i // MOBA_BLOCK == qry_i // MOBA_BLOCK)
    start_own = pl.multiple_of(first_blk * MOBA_BLOCK, ATT_Q_TILE)

    state = []
    for par in range(2):
        s = _dot(kx_ref[par, pl.ds(start_own, ATT_Q_TILE), :], qx_own[par])
        s = jnp.where(own_mask, s, NEG)
        m = jnp.max(s, axis=0, keepdims=True)
        p = jnp.exp2(s - m)
        acc = _dot(vt_ref[par, :, pl.ds(start_own, ATT_Q_TILE)], p.astype(BF16))
        state += [m, acc]

    n_groups = lax.shift_right_logical(first_blk + (ATT_Q_BLOCKS - 1) + (ATT_GROUP - 1), ATT_GROUP_SHIFT)
    last_group = N_BLOCKS // ATT_GROUP - 1

    def scores(g, s_ref, mx_ref):
        start = pl.multiple_of(g * ATT_GROUP_KEYS, ATT_GROUP_KEYS)
        for par in range(2):
            s = _dot(kx_ref[par, pl.ds(start, ATT_GROUP_KEYS), :], qx_past[par])
            s_ref[par] = s
            mx_ref[par] = jnp.max(s, axis=0, keepdims=True)

    def consume(g, s_ref, mx_ref, carry):
        start = pl.multiple_of(g * ATT_GROUP_KEYS, ATT_GROUP_KEYS)
        out = []
        for par in range(2):
            m, acc = carry[2 * par:2 * par + 2]
            m_new = jnp.maximum(m, mx_ref[par])
            a = jnp.exp2(m - m_new)
            p = jnp.exp2(s_ref[par] - m_new)
            acc = a * acc + _dot(vt_ref[par, :, pl.ds(start, ATT_GROUP_KEYS)], p.astype(BF16))
            out += [m_new, acc]
        return tuple(out)

    scores(0, s0_ref, mx0_ref)

    def body(g, carry):
        g_next = jnp.minimum(g + 1, last_group)

        def even(c):
            scores(g_next, s1_ref, mx1_ref)
            return consume(g, s0_ref, mx0_ref, c)

        def odd(c):
            scores(g_next, s0_ref, mx0_ref)
            return consume(g, s1_ref, mx1_ref, c)

        return lax.cond((g & 1) == 0, even, odd, carry)

    state = lax.fori_loop(0, n_groups, body, tuple(state))
    o_t = jnp.concatenate([acc[0:HEAD_DIM] / acc[HEAD_DIM:HEAD_DIM + 1] for acc in (state[1], state[3])],
                          axis=0)
    o_ref[...] = o_t.T.astype(BF16)


def _attn(q, kx, vt, kmean):
    n_pairs = ATT_WIDTH // LANES
    return pl.pallas_call(
        _attn_kernel,
        grid=(n_pairs, SEQ // ATT_Q_TILE),
        in_specs=[pl.BlockSpec((ATT_Q_TILE, LANES), lambda j, t: (t, j)),
                  pl.BlockSpec((2, SEQ, LANES), lambda j, t: (j, 0, 0)),
                  pl.BlockSpec((2, VT_ROWS, SEQ), lambda j, t: (j, 0, 0)),
                  pl.BlockSpec((N_BLOCKS, LANES), lambda j, t: (0, j))],
        out_specs=pl.BlockSpec((ATT_Q_TILE, LANES), lambda j, t: (t, j)),
        out_shape=jax.ShapeDtypeStruct((SEQ, ATT_WIDTH), BF16),
        scratch_shapes=([pltpu.VMEM((2, ATT_GROUP_KEYS, ATT_Q_TILE), F32)] * 2
                        + [pltpu.VMEM((2, 1, ATT_Q_TILE), F32)] * 2),
        compiler_params=pltpu.CompilerParams(dimension_semantics=("arbitrary", "arbitrary"),
                                             vmem_limit_bytes=VMEM_LIMIT),
        name="attn",
    )(q, kx, vt, kmean)


def _outproj_kernel(h_ref, ya_ref, ys_ref, yc_ref, wo_ref, g_ref, b_ref, wr_ref, rb_ref,
                    h1_ref, h1b_ref, grp_ref, rank_ref, wk_ref, cnt_ref):
    mix = (_dot(ya_ref[...], wo_ref[0:ATT_WIDTH, :])
           + _dot(ys_ref[...], wo_ref[ATT_WIDTH:ATT_WIDTH + SG_WIDTH, :])
           + _dot(yc_ref[...], wo_ref[ATT_WIDTH + SG_WIDTH:, :]))
    h1 = _layer_norm(DEEPNORM_ALPHA * h_ref[...] + mix, g_ref[...], b_ref[...])
    h1_ref[...] = h1
    h1b_ref[...] = h1.astype(BF16)

    w_hi, w_lo = _split_bf16(wr_ref[...])
    h_hi, h_lo = _split_bf16(h1)
    both = _dot_nt(jnp.concatenate([w_hi, w_lo], axis=0), h_hi)
    logits = both[0:N_EXPERTS] + both[N_EXPERTS:] + _dot_nt(w_hi, h_lo)
    scores = jax.nn.sigmoid(logits)
    biased = scores + rb_ref[...]
    sc = [scores[e:e + 1, :] for e in range(N_EXPERTS)]
    bi = [biased[e:e + 1, :] for e in range(N_EXPERTS)]

    best = None
    for g in range(N_EXPERT_GROUPS):
        a0, a1, a2, a3 = bi[4 * g:4 * g + 4]
        hi01, lo01 = jnp.maximum(a0, a1), jnp.minimum(a0, a1)
        hi23, lo23 = jnp.maximum(a2, a3), jnp.minimum(a2, a3)
        top1 = jnp.maximum(hi01, hi23)
        top2 = jnp.maximum(jnp.minimum(hi01, hi23), jnp.maximum(lo01, lo23))
        gs = top1 + top2
        if best is None:
            best, grp = gs, jnp.zeros_like(gs, dtype=jnp.int32)
        else:
            upd = gs > best
            best = jnp.where(upd, gs, best)
            grp = jnp.where(upd, g, grp)

    def in_group(vals, k):
        out = vals[k]
        for g in range(1, N_EXPERT_GROUPS):
            out = jnp.where(grp == g, vals[4 * g + k], out)
        return out

    vb = [in_group(bi, k) for k in range(EXPERTS_PER_GROUP)]
    vs = [in_group(sc, k) for k in range(EXPERTS_PER_GROUP)]
    m1, i1, s1 = vb[0], jnp.zeros_like(grp), vs[0]
    for k in range(1, EXPERTS_PER_GROUP):
        upd = vb[k] > m1
        m1 = jnp.where(upd, vb[k], m1)
        i1 = jnp.where(upd, k, i1)
        s1 = jnp.where(upd, vs[k], s1)
    m2 = jnp.full_like(m1, -jnp.inf)
    i2 = jnp.zeros_like(grp)
    s2 = jnp.zeros_like(s1)
    for k in range(EXPERTS_PER_GROUP):
        upd = (i1 != k) & (vb[k] > m2)
        m2 = jnp.where(upd, vb[k], m2)
        i2 = jnp.where(upd, k, i2)
        s2 = jnp.where(upd, vs[k], s2)
    tot = s1 + s2
    row = lax.broadcasted_iota(jnp.int32, (GATE_ROWS, ROW_TILE), 0)
    rest = (jnp.where(row % EXPERTS_PER_GROUP == i1, s1 / tot, 0.0)
            + jnp.where(row % EXPERTS_PER_GROUP == i2, s2 / tot, 0.0))
    pieces = jnp.zeros((GATE_ROWS, ROW_TILE), F32)
    for i in range(GATE_PIECES):
        piece = rest.astype(BF16).astype(F32)
        pieces = jnp.where(row // EXPERTS_PER_GROUP == i, piece, pieces)
        rest = rest - piece
    wk_ref[...] = pieces.astype(BF16)
    grp_ref[...] = grp
    k_idx = lax.broadcasted_iota(jnp.int32, (EXPERTS_PER_GROUP, ROW_TILE), 0)

    @pl.when(pl.program_id(0) == 0)
    def _():
        cnt_ref[...] = jnp.zeros_like(cnt_ref)

    member = (k_idx == grp).astype(F32)
    t_r = lax.broadcasted_iota(jnp.int32, (ROW_TILE, ROW_TILE), 0)
    t_c = lax.broadcasted_iota(jnp.int32, (ROW_TILE, ROW_TILE), 1)
    earlier = _dot(member.astype(BF16), (t_r < t_c).astype(BF16)) + cnt_ref[...]
    rank_ref[...] = jnp.sum(member * earlier, axis=0, keepdims=True).astype(jnp.int32)
    cnt_ref[...] += jnp.sum(member, axis=1, keepdims=True)


def _outproj(h, y_att, y_sg, y_cv, w_out, ln_g, ln_b, w_router_t, router_bias):
    row = lambda w: pl.BlockSpec((ROW_TILE, w), lambda i: (i, 0))
    full = lambda shape: pl.BlockSpec(shape, lambda i: (0,) * len(shape))
    lane_row = lambda n: pl.BlockSpec((n, ROW_TILE), lambda i: (0, i))
    return pl.pallas_call(
        _outproj_kernel,
        grid=(SEQ // ROW_TILE,),
        in_specs=[row(D_MODEL), row(ATT_WIDTH), row(SG_WIDTH), row(CONV_WIDTH), full((D_MODEL, D_MODEL)),
                  full((1, D_MODEL)), full((1, D_MODEL)), full((N_EXPERTS, D_MODEL)), full((N_EXPERTS, 1))],
        out_specs=[row(D_MODEL), row(D_MODEL), lane_row(1), lane_row(1), lane_row(GATE_ROWS)],
        out_shape=[jax.ShapeDtypeStruct((SEQ, D_MODEL), F32),
                   jax.ShapeDtypeStruct((SEQ, D_MODEL), BF16),
                   jax.ShapeDtypeStruct((1, SEQ), jnp.int32),
                   jax.ShapeDtypeStruct((1, SEQ), jnp.int32),
                   jax.ShapeDtypeStruct((GATE_ROWS, SEQ), BF16)],
        scratch_shapes=[pltpu.VMEM((N_EXPERT_GROUPS, ROW_TILE), F32)],
        compiler_params=pltpu.CompilerParams(dimension_semantics=("arbitrary",), vmem_limit_bytes=VMEM_LIMIT),
        name="outproj",
    )(h, y_att, y_sg, y_cv, w_out, ln_g, ln_b, w_router_t, router_bias)


PAIR_VALID, PAIR_FIRST, PAIR_LAST, PAIR_FILL = 1, 2, 4, 8


def _moe_plan(grp, rank):
    n_src = SEQ // MOE_SRC_TILE
    member = (grp[:, None] == jnp.arange(N_EXPERT_GROUPS)[None, :]).astype(jnp.int32)
    cnt = member.reshape(n_src, MOE_SRC_TILE, N_EXPERT_GROUPS).sum(axis=1).T
    start = jnp.cumsum(cnt, axis=1) - cnt
    tiles = (cnt.sum(axis=1) + MOE_TILE - 1) // MOE_TILE
    tile0 = jnp.cumsum(tiles) - tiles
    pos = tile0[grp] * MOE_TILE + rank

    lo = jnp.arange(SEQ // MOE_TILE)[None, :, None] * MOE_TILE
    overlap = (cnt[:, None, :] > 0) & (start[:, None, :] < lo + MOE_TILE) & ((start + cnt)[:, None, :] > lo)
    n_pairs = overlap.sum()
    idx = jnp.arange(MOE_MAX_PAIRS)
    keep = jnp.minimum(idx, n_pairs - 1)

    def pair_list(mask, order):
        running = jnp.cumsum(mask.reshape(-1).astype(jnp.int32))
        flat = (running[None, :] <= keep[:, None]).sum(axis=1)
        ix = jnp.unravel_index(flat, mask.shape)
        g, il, r = (ix[order.index(a)] for a in "gtr")
        tile = tile0[g] + il
        key = tile if order[0] == "g" else r
        valid = idx < n_pairs
        first = valid & ((idx == 0) | (key != jnp.roll(key, 1)))
        last = valid & ((idx == n_pairs - 1) | (key != jnp.roll(key, -1)))
        flags = PAIR_VALID * valid + PAIR_FIRST * first + PAIR_LAST * last
        if order[0] == "g":
            spare_tile = tiles.sum() + idx - n_pairs
            fill = (idx >= n_pairs) & (spare_tile < MOE_MAX_TILES)
            tile = jnp.where(idx >= n_pairs, jnp.minimum(spare_tile, MOE_MAX_TILES - 1), tile)
            flags = flags + PAIR_FILL * fill
        return tile.astype(jnp.int32), r.astype(jnp.int32), g.astype(jnp.int32), flags.astype(jnp.int32)

    return pos, pair_list(overlap, "gtr")


def _moe_ffn_kernel(pt_ref, pr_ref, pg_ref, pf_ref, h_ref, pos_ref, wk_ref, wg_ref, wu_ref, wd_ref, y_ref,
                    x_ref, w_ref):
    p = pl.program_id(0)
    flags = pf_ref[p]

    @pl.when((flags & PAIR_FIRST) != 0)
    def _():
        x_ref[...] = jnp.zeros_like(x_ref)
        w_ref[...] = jnp.zeros_like(w_ref)

    @pl.when((flags & PAIR_VALID) != 0)
    def _():
        rows = pt_ref[p] * MOE_TILE + lax.broadcasted_iota(jnp.int32, (MOE_TILE, MOE_SRC_TILE), 0)
        onehot = jnp.where(pos_ref[...] == rows, 1.0, 0.0).astype(BF16)
        x_ref[...] += _dot(onehot, h_ref[...])
        w_ref[...] += _dot_nt(onehot, wk_ref[...])

    @pl.when((flags & PAIR_LAST) != 0)
    def _():
        x = x_ref[...].astype(BF16)
        w = w_ref[...]
        w = sum(w[:, i * EXPERTS_PER_GROUP:(i + 1) * EXPERTS_PER_GROUP] for i in range(GATE_PIECES))
        y = jnp.zeros((MOE_TILE, D_MODEL), F32)
        for k in range(EXPERTS_PER_GROUP):
            a = jax.nn.silu(_dot(x, wg_ref[0, k])) * _dot(x, wu_ref[0, k])
            y = y + w[:, k:k + 1] * _dot(a.astype(BF16), wd_ref[0, k])
        y_ref[...] = y

    @pl.when((flags & PAIR_FILL) != 0)
    def _():
        y_ref[...] = jnp.zeros_like(y_ref)


def _moe_ffn(layer, plan, h_bf16, pos, wk, w_gate, w_up, w_down):
    tile, src, grp, flags = plan
    src_rows = lambda w: pl.BlockSpec((MOE_SRC_TILE, w), lambda p, pt, pr, pg, pf: (pr[p], 0))
    src_lanes = lambda n: pl.BlockSpec((n, MOE_SRC_TILE), lambda p, pt, pr, pg, pf: (0, pr[p]))
    experts = lambda a, b: pl.BlockSpec((1, EXPERTS_PER_GROUP, a, b),
                                        lambda p, pt, pr, pg, pf: (layer, pg[p], 0, 0))
    return pl.pallas_call(
        _moe_ffn_kernel,
        grid_spec=pltpu.PrefetchScalarGridSpec(
            num_scalar_prefetch=4,
            grid=(MOE_MAX_PAIRS,),
            in_specs=[src_rows(D_MODEL), src_lanes(1), src_lanes(GATE_ROWS),
                      experts(D_MODEL, D_EXPERT), experts(D_MODEL, D_EXPERT), experts(D_EXPERT, D_MODEL)],
            out_specs=pl.BlockSpec((MOE_TILE, D_MODEL), lambda p, pt, pr, pg, pf: (pt[p], 0)),
            scratch_shapes=[pltpu.VMEM((MOE_TILE, D_MODEL), F32), pltpu.VMEM((MOE_TILE, GATE_ROWS), F32)]),
        out_shape=jax.ShapeDtypeStruct((MOE_SORTED_ROWS, D_MODEL), F32),
        compiler_params=pltpu.CompilerParams(dimension_semantics=("arbitrary",), vmem_limit_bytes=VMEM_LIMIT),
        name="moe_ffn",
    )(tile, src, grp, flags, h_bf16, pos, wk, w_gate, w_up, w_down)


def _gather_copy(y_hbm, buf, sem, slot, src_row, dst_row, n_rows):
    return pltpu.make_async_copy(y_hbm.at[pl.ds(src_row, n_rows), :], buf.at[slot, pl.ds(dst_row, n_rows), :],
                                 sem.at[slot])


def _moe_combine_kernel(pos_ref, y_hbm, h_ref, g_ref, b_ref, o_ref, buf, sem):
    i = pl.program_id(0)
    slot = i & 1

    def start_tile(tile, slot):
        def body(r, carry):
            _gather_copy(y_hbm, buf, sem, slot, pos_ref[tile * MOE_SRC_TILE + r], r, 1).start()
            return carry
        lax.fori_loop(0, MOE_SRC_TILE, body, 0, unroll=8)

    @pl.when(i == 0)
    def _():
        start_tile(0, 0)

    @pl.when(i + 1 < pl.num_programs(0))
    def _():
        start_tile(i + 1, 1 - slot)

    _gather_copy(y_hbm, buf, sem, slot, 0, 0, MOE_SRC_TILE).wait()
    o_ref[...] = _layer_norm(DEEPNORM_ALPHA * h_ref[...] + buf[slot], g_ref[...], b_ref[...])


def _moe_combine(y_sorted, pos, h, ln_g, ln_b):
    src_rows = lambda w: pl.BlockSpec((MOE_SRC_TILE, w), lambda i, pos: (i, 0))
    const = lambda shape: pl.BlockSpec(shape, lambda i, pos: (0,) * len(shape))
    return pl.pallas_call(
        _moe_combine_kernel,
        grid_spec=pltpu.PrefetchScalarGridSpec(
            num_scalar_prefetch=1,
            grid=(SEQ // MOE_SRC_TILE,),
            in_specs=[pl.BlockSpec(memory_space=pl.ANY), src_rows(D_MODEL), const((1, D_MODEL)),
                      const((1, D_MODEL))],
            out_specs=src_rows(D_MODEL),
            scratch_shapes=[pltpu.VMEM((2, MOE_SRC_TILE, D_MODEL), F32), pltpu.SemaphoreType.DMA((2,))]),
        out_shape=jax.ShapeDtypeStruct((SEQ, D_MODEL), F32),
        compiler_params=pltpu.CompilerParams(dimension_semantics=("arbitrary",), vmem_limit_bytes=VMEM_LIMIT),
        name="moe_combine",
    )(pos, y_sorted, h, ln_g, ln_b)


def _rope_tables():
    half = HEAD_DIM // 2
    inv_freq = ROPE_THETA ** (-jnp.arange(half, dtype=F32) / half)
    ang = jnp.arange(SEQ, dtype=F32)[:, None] * inv_freq[None, :]
    cos, sin = jnp.cos(ang), jnp.sin(ang)
    reps = LANES // HEAD_DIM
    return jnp.tile(jnp.concatenate([cos, cos], axis=1), (1, reps)), jnp.tile(jnp.concatenate([-sin, sin], axis=1), (1, reps))


def kernel(x, ln_in_g, ln_in_b, w_in, w_out, sg_ln_g, sg_ln_b, sg_w, sg_b, conv_w, conv_b, conv_ln_g, conv_ln_b, ln_mix_g, ln_mix_b, w_router, router_bias, w_gate, w_up, w_down, ln_ffn_g, ln_ffn_b):
    assert x.shape == (1, SEQ, D_MODEL) and w_in.shape == (DEPTH, D_MODEL, IN_COLS)
    cos, sin = _rope_tables()
    row = lambda a: a.reshape(1, -1)
    w_router_t = w_router.T
    rb = router_bias.reshape(N_EXPERTS, 1)
    w_gate_bf16, w_up_bf16, w_down_bf16 = w_gate.astype(BF16), w_up.astype(BF16), w_down.astype(BF16)
    h = x.reshape(SEQ, D_MODEL)
    for l in range(DEPTH):
        sg_bias = jnp.repeat(sg_b[l].T, SG_GROUP_DIM, axis=1)
        w_p = jnp.concatenate([w_in[l][:, :V_OFF], w_in[l][:, SGU_OFF:]], axis=1).astype(BF16)
        w_vt = w_in[l][:, V_OFF:SGU_OFF].T.astype(BF16)
        args = (w_p, w_vt, cos, sin, row(sg_ln_g[l]), row(sg_ln_b[l]), sg_w[l], sg_bias,
                conv_w[l], row(conv_b[l]), row(conv_ln_g[l]), row(conv_ln_b[l]))
        if l == 0:
            h, q, kx, vt, kmean, y_sg, y_cv = _front(True, h, row(ln_in_g), row(ln_in_b), *args)
        else:
            q, kx, vt, kmean, y_sg, y_cv = _front(False, h, None, None, *args)
        y_att = _attn(q, kx, vt, kmean[:, 0, :])
        h, h_bf16, grp, rank, wk = _outproj(h, y_att, y_sg, y_cv, w_out[l].astype(BF16), row(ln_mix_g[l]),
                                            row(ln_mix_b[l]), w_router_t, rb)
        pos, by_tile = _moe_plan(grp[0], rank[0])
        y_sorted = _moe_ffn(l, by_tile, h_bf16, pos.reshape(1, SEQ), wk, w_gate_bf16, w_up_bf16, w_down_bf16)
        h = _moe_combine(y_sorted, pos, h, row(ln_ffn_g[l]), row(ln_ffn_b[l]))
    return h.reshape(1, SEQ, D_MODEL)
```

```python
import functools
import math

import jax
import jax.numpy as jnp
from jax import lax
from jax.experimental import pallas as pl
from jax.experimental.pallas import tpu as pltpu

D_MODEL = 1024
SEQ = 16384
DEPTH = 2
HEAD_DIM = 64
ATT_WIDTH = 512
ATT_HEADS = 8
SG_WIDTH = 256
SG_GROUPS = 4
SG_GROUP_DIM = 64
CONV_WIDTH = 256
K_OFF = 512
V_OFF = 1024
SGU_OFF = 1536
SGV_OFF = 1792
CA_OFF = 2048
CG_OFF = 2304
IN_COLS = 2560
P_SGU = SGU_OFF - ATT_WIDTH
P_SGV = SGV_OFF - ATT_WIDTH
P_CA = CA_OFF - ATT_WIDTH
P_CG = CG_OFF - ATT_WIDTH
P_COLS = IN_COLS - ATT_WIDTH
MOBA_BLOCK = 256
MOBA_TOPK = 3
N_BLOCKS = SEQ // MOBA_BLOCK
SG_CHUNK = 128
CONV_TAPS = 31
ROPE_THETA = 10000.0
N_EXPERTS = 16
N_EXPERT_GROUPS = 4
EXPERTS_PER_GROUP = 4
D_EXPERT = 512
LN_EPS = 1e-5
NEG = -1e30
DEEPNORM_ALPHA = (2.0 * DEPTH) ** 0.25

LANES = 128
SUBLANES = 8
ROW_TILE = MOBA_BLOCK
CONV_HALO = 32
LOG2_E = math.log2(math.e)
VT_ROWS = HEAD_DIM + 16
ATT_GROUP_SHIFT = 2
ATT_GROUP = 1 << ATT_GROUP_SHIFT
ATT_Q_BLOCKS = 2
ATT_Q_TILE = ATT_Q_BLOCKS * MOBA_BLOCK
ATT_GROUP_KEYS = ATT_GROUP * MOBA_BLOCK
MOE_ROW_WIDTH = D_MODEL + LANES
GATE_PIECES = 3
GATE_ROWS = 16
MOE_TILE = 256
MOE_SRC_TILE = 512
MOE_MAX_TILES = SEQ // MOE_TILE + N_EXPERT_GROUPS
MOE_MAX_PAIRS = MOE_MAX_TILES + N_EXPERT_GROUPS * (SEQ // MOE_SRC_TILE)
MOE_SORTED_ROWS = MOE_MAX_TILES * MOE_TILE
VMEM_LIMIT = 48 * 1024 * 1024

F32 = jnp.float32
BF16 = jnp.bfloat16


def _dot(a, b):
    return jnp.dot(a, b, preferred_element_type=F32)


def _dot_nt(a, b):
    return lax.dot_general(a, b, (((1,), (1,)), ((), ())), preferred_element_type=F32)


def _layer_norm(x, g, b):
    mu = jnp.mean(x, axis=-1, keepdims=True)
    xc = x - mu
    var = jnp.mean(xc * xc, axis=-1, keepdims=True)
    return xc * lax.rsqrt(var + LN_EPS) * g + b


def _split_bf16(x):
    hi = x.astype(BF16)
    lo = (x - hi.astype(F32)).astype(BF16)
    return hi, lo


def _front_kernel(apply_ln, *refs):
    if apply_ln:
        (x_ref, lng_ref, lnb_ref, w_ref, wvt_ref, cos_ref, sin_ref, sglng_ref, sglnb_ref, sgw_ref, sgb_ref,
         cw_ref, cb_ref, clng_ref, clnb_ref,
         h_ref, q_ref, kx_ref, vt_ref, km_ref, ysg_ref, ycv_ref, ext_ref) = refs
    else:
        (x_ref, w_ref, wvt_ref, cos_ref, sin_ref, sglng_ref, sglnb_ref, sgw_ref, sgb_ref,
         cw_ref, cb_ref, clng_ref, clnb_ref,
         q_ref, kx_ref, vt_ref, km_ref, ysg_ref, ycv_ref, ext_ref) = refs
    i = pl.program_id(0)
    h = x_ref[...]
    if apply_ln:
        h = _layer_norm(h, lng_ref[...], lnb_ref[...])
        h_ref[...] = h
    hb = h.astype(BF16)
    p = _dot(hb, w_ref[...])
    vt_ref[:, 0:HEAD_DIM, :] = _dot_nt(wvt_ref[...], hb).reshape(ATT_HEADS, HEAD_DIM, ROW_TILE).astype(BF16)
    vt_ref[:, HEAD_DIM:, :] = jnp.ones((ATT_HEADS, VT_ROWS - HEAD_DIM, ROW_TILE), BF16)

    lane = lax.broadcasted_iota(jnp.int32, (ROW_TILE, LANES), 1)
    first_half = (lane % HEAD_DIM) < (HEAD_DIM // 2)
    low_head = lane < HEAD_DIM
    cos = cos_ref[...]
    sin = sin_ref[...]

    def rot(x):
        swapped = jnp.where(first_half, pltpu.roll(x, LANES - HEAD_DIM // 2, 1), pltpu.roll(x, HEAD_DIM // 2, 1))
        return x * cos + swapped * sin

    scale = HEAD_DIM ** -0.5 * LOG2_E
    for j in range(ATT_WIDTH // LANES):
        sl = slice(j * LANES, (j + 1) * LANES)
        q_ref[:, sl] = (rot(p[:, sl]) * scale).astype(BF16)
        kr = rot(p[:, K_OFF + j * LANES:K_OFF + (j + 1) * LANES])
        km_ref[0, :, sl] = jnp.broadcast_to(jnp.mean(kr, axis=0, keepdims=True), (SUBLANES, LANES))
        kx_ref[2 * j] = jnp.where(low_head, kr, (lane - HEAD_DIM == i).astype(F32)).astype(BF16)
        kx_ref[2 * j + 1] = jnp.where(low_head, (lane == i).astype(F32), kr).astype(BF16)

    u = jax.nn.gelu(p[:, P_SGU:P_SGV])
    vv = _layer_norm(jax.nn.gelu(p[:, P_SGV:P_CA]), sglng_ref[...], sglnb_ref[...]).astype(BF16)
    r_i = lax.broadcasted_iota(jnp.int32, (SG_CHUNK, SG_CHUNK), 0)
    c_i = lax.broadcasted_iota(jnp.int32, (SG_CHUNK, SG_CHUNK), 1)
    lane_sg = lax.broadcasted_iota(jnp.int32, (SG_CHUNK, SG_WIDTH), 1)
    for c in range(ROW_TILE // SG_CHUNK):
        rows = slice(c * SG_CHUNK, (c + 1) * SG_CHUNK)
        mixed = sgb_ref[...]
        for g in range(SG_GROUPS):
            wg = jnp.where(c_i <= r_i, sgw_ref[g], 0.0).astype(BF16)
            full = _dot(wg, vv[rows])
            mixed = mixed + jnp.where(lane_sg // SG_GROUP_DIM == g, full, 0.0)
        ysg_ref[rows, :] = (u[rows] * mixed).astype(BF16)

    @pl.when(i == 0)
    def _():
        ext_ref[0:CONV_HALO, :] = jnp.zeros((CONV_HALO, CONV_WIDTH), F32)

    ext_ref[CONV_HALO:, :] = p[:, P_CA:P_CG] * jax.nn.sigmoid(p[:, P_CG:P_COLS])
    acc = jnp.zeros((ROW_TILE, CONV_WIDTH), F32) + cb_ref[...]
    first = CONV_HALO - (CONV_TAPS - 1)
    ext = ext_ref[...]
    n_ext = CONV_HALO + ROW_TILE
    for shift in range(SUBLANES):
        shifted = ext if shift == 0 else pltpu.roll(ext, n_ext - shift, 0)
        for o in range(first + (shift - first) % SUBLANES, first + CONV_TAPS, SUBLANES):
            acc = acc + cw_ref[o - first:o - first + 1, :] * shifted[o - shift:o - shift + ROW_TILE]
    ext_ref[0:CONV_HALO, :] = ext_ref[ROW_TILE:ROW_TILE + CONV_HALO, :]
    hc = _layer_norm(acc, clng_ref[...], clnb_ref[...])
    ycv_ref[...] = (hc * jax.nn.sigmoid(hc)).astype(BF16)


def _front(apply_ln, x, ln_g, ln_b, w_p, w_vt, cos, sin, sg_ln_g, sg_ln_b, sg_w, sg_bias, conv_w, conv_b,
           conv_ln_g, conv_ln_b):
    n_tiles = SEQ // ROW_TILE
    row = lambda w: pl.BlockSpec((ROW_TILE, w), lambda i: (i, 0))
    full = lambda shape: pl.BlockSpec(shape, lambda i: (0,) * len(shape))
    in_specs = [row(D_MODEL)]
    args = [x]
    if apply_ln:
        in_specs += [full((1, D_MODEL)), full((1, D_MODEL))]
        args += [ln_g, ln_b]
    in_specs += [full((D_MODEL, P_COLS)), full((ATT_WIDTH, D_MODEL)), row(LANES), row(LANES),
                 full((1, SG_WIDTH)), full((1, SG_WIDTH)),
                 full((SG_GROUPS, SG_CHUNK, SG_CHUNK)), full((SG_CHUNK, SG_WIDTH)),
                 full((CONV_TAPS, CONV_WIDTH)), full((1, CONV_WIDTH)), full((1, CONV_WIDTH)),
                 full((1, CONV_WIDTH))]
    args += [w_p, w_vt, cos, sin, sg_ln_g, sg_ln_b, sg_w, sg_bias, conv_w, conv_b, conv_ln_g, conv_ln_b]
    out_shape = [jax.ShapeDtypeStruct((SEQ, ATT_WIDTH), BF16),
                 jax.ShapeDtypeStruct((ATT_HEADS, SEQ, LANES), BF16),
                 jax.ShapeDtypeStruct((ATT_HEADS, VT_ROWS, SEQ), BF16),
                 jax.ShapeDtypeStruct((N_BLOCKS, SUBLANES, ATT_WIDTH), F32),
                 jax.ShapeDtypeStruct((SEQ, SG_WIDTH), BF16),
                 jax.ShapeDtypeStruct((SEQ, CONV_WIDTH), BF16)]
    out_specs = [row(ATT_WIDTH), pl.BlockSpec((ATT_HEADS, ROW_TILE, LANES), lambda i: (0, i, 0)),
                 pl.BlockSpec((ATT_HEADS, VT_ROWS, ROW_TILE), lambda i: (0, 0, i)),
                 pl.BlockSpec((1, SUBLANES, ATT_WIDTH), lambda i: (i, 0, 0)),
                 row(SG_WIDTH), row(CONV_WIDTH)]
    if apply_ln:
        out_shape = [jax.ShapeDtypeStruct((SEQ, D_MODEL), F32)] + out_shape
        out_specs = [row(D_MODEL)] + out_specs
    return pl.pallas_call(
        functools.partial(_front_kernel, apply_ln),
        grid=(n_tiles,),
        in_specs=in_specs,
        out_specs=out_specs,
        out_shape=out_shape,
        scratch_shapes=[pltpu.VMEM((CONV_HALO + ROW_TILE, CONV_WIDTH), F32)],
        compiler_params=pltpu.CompilerParams(dimension_semantics=("arbitrary",), vmem_limit_bytes=VMEM_LIMIT),
        name="front_ln" if apply_ln else "front",
    )(*args)


def _attn_kernel(q_ref, kx_ref, vt_ref, km_ref, o_ref, s0_ref, s1_ref, mx0_ref, mx1_ref):
    first_blk = pl.program_id(1) * ATT_Q_BLOCKS
    q = q_ref[...]

    km = km_ref[...]
    km_lane = lax.broadcasted_iota(jnp.int32, (N_BLOCKS, LANES), 1)
    km_rows = jnp.concatenate([jnp.where(km_lane >= HEAD_DIM, km, 0.0), jnp.where(km_lane < HEAD_DIM, km, 0.0)],
                              axis=0)
    km_hi, km_lo = _split_bf16(km_rows)
    eye = (lax.broadcasted_iota(jnp.int32, (LANES, LANES), 0)
           == lax.broadcasted_iota(jnp.int32, (LANES, LANES), 1)).astype(F32).astype(BF16)
    res = _dot_nt(jnp.concatenate([km_hi, km_lo, eye], axis=0), q)
    q_t = res[2 * LANES:].astype(BF16)
    gate = (res[0:LANES] + res[LANES:2 * LANES]).reshape(2, N_BLOCKS, ATT_Q_TILE)
    blk = lax.broadcasted_iota(jnp.int32, (2, N_BLOCKS, ATT_Q_TILE), 1)
    own = first_blk + lax.broadcasted_iota(jnp.int32, (2, N_BLOCKS, ATT_Q_TILE), 2) // MOBA_BLOCK
    gate = jnp.where(blk < own, gate, NEG)

    sel = jnp.zeros((2, N_BLOCKS, ATT_Q_TILE), jnp.bool_)
    blk_f = blk.astype(F32)
    for r in range(MOBA_TOPK):
        best = jnp.max(gate, axis=1, keepdims=True)
        first = jnp.min(jnp.where(gate == best, blk_f, float(N_BLOCKS)), axis=1, keepdims=True)
        pick = blk_f == first
        sel = sel | (pick & (own > r))
        gate = jnp.where(pick, -jnp.inf, gate)
    bias_t = jnp.where(sel, 0.0, NEG).astype(BF16).reshape(LANES, ATT_Q_TILE)
    low = lax.broadcasted_iota(jnp.int32, (LANES, ATT_Q_TILE), 0) < HEAD_DIM
    zero = jnp.zeros_like(q_t)
    qx_past = (jnp.where(low, q_t, bias_t), jnp.where(low, bias_t, q_t))
    qx_own = (jnp.where(low, q_t, zero), jnp.where(low, zero, q_t))

    key_i = lax.broadcasted_iota(jnp.int32, (ATT_Q_TILE, ATT_Q_TILE), 0)
    qry_i = lax.broadcasted_iota(jnp.int32, (ATT_Q_TILE, ATT_Q_TILE), 1)
    own_mask = (key_i <= qry_i) & (key_i // MOBA_BLOCK == qry_i // MOBA_BLOCK)
    start_own = pl.multiple_of(first_blk * MOBA_BLOCK, ATT_Q_TILE)

    state = []
    for par in range(2):
        s = _dot(kx_ref[par, pl.ds(start_own, ATT_Q_TILE), :], qx_own[par])
        s = jnp.where(own_mask, s, NEG)
        m = jnp.max(s, axis=0, keepdims=True)
        p = jnp.exp2(s - m)
        acc = _dot(vt_ref[par, :, pl.ds(start_own, ATT_Q_TILE)], p.astype(BF16))
        state += [m, acc]

    n_groups = lax.shift_right_logical(first_blk + (ATT_Q_BLOCKS - 1) + (ATT_GROUP - 1), ATT_GROUP_SHIFT)

    def scores(g, s_ref, mx_ref):
        start = pl.multiple_of(g * ATT_GROUP_KEYS, ATT_GROUP_KEYS)
        for par in range(2):
            s = _dot(kx_ref[par, pl.ds(start, ATT_GROUP_KEYS), :], qx_past[par])
            s_ref[par] = s
            mx_ref[par] = jnp.max(s, axis=0, keepdims=True)

    def consume(g, s_ref, mx_ref, carry):
        start = pl.multiple_of(g * ATT_GROUP_KEYS, ATT_GROUP_KEYS)
        out = []
        for par in range(2):
            m, acc = carry[2 * par:2 * par + 2]
            m_new = jnp.maximum(m, mx_ref[par])
            a = jnp.exp2(m - m_new)
            p = jnp.exp2(s_ref[par] - m_new)
            acc = a * acc + _dot(vt_ref[par, :, pl.ds(start, ATT_GROUP_KEYS)], p.astype(BF16))
            out += [m_new, acc]
        return tuple(out)

    scores(0, s0_ref, mx0_ref)

    def body(g, carry):
        def even(c):
            scores(g + 1, s1_ref, mx1_ref)
            return consume(g, s0_ref, mx0_ref, c)

        def odd(c):
            scores(g + 1, s0_ref, mx0_ref)
            return consume(g, s1_ref, mx1_ref, c)

        return lax.cond((g & 1) == 0, even, odd, carry)

    last = n_groups - 1
    state = lax.fori_loop(0, last, body, tuple(state))
    state = lax.cond((last & 1) == 0, lambda c: consume(last, s0_ref, mx0_ref, c),
                     lambda c: consume(last, s1_ref, mx1_ref, c), state)
    o_t = jnp.concatenate([acc[0:HEAD_DIM] / acc[HEAD_DIM:HEAD_DIM + 1] for acc in (state[1], state[3])],
                          axis=0)
    o_ref[...] = o_t.T.astype(BF16)


def _attn(q, kx, vt, kmean):
    n_pairs = ATT_WIDTH // LANES
    return pl.pallas_call(
        _attn_kernel,
        grid=(n_pairs, SEQ // ATT_Q_TILE),
        in_specs=[pl.BlockSpec((ATT_Q_TILE, LANES), lambda j, t: (t, j)),
                  pl.BlockSpec((2, SEQ, LANES), lambda j, t: (j, 0, 0)),
                  pl.BlockSpec((2, VT_ROWS, SEQ), lambda j, t: (j, 0, 0)),
                  pl.BlockSpec((N_BLOCKS, LANES), lambda j, t: (0, j))],
        out_specs=pl.BlockSpec((ATT_Q_TILE, LANES), lambda j, t: (t, j)),
        out_shape=jax.ShapeDtypeStruct((SEQ, ATT_WIDTH), BF16),
        scratch_shapes=([pltpu.VMEM((2, ATT_GROUP_KEYS, ATT_Q_TILE), F32)] * 2
                        + [pltpu.VMEM((2, 1, ATT_Q_TILE), F32)] * 2),
        compiler_params=pltpu.CompilerParams(dimension_semantics=("arbitrary", "arbitrary"),
                                             vmem_limit_bytes=VMEM_LIMIT),
        name="attn",
    )(q, kx, vt, kmean)


def _outproj_kernel(h_ref, ya_ref, ys_ref, yc_ref, wo_ref, g_ref, b_ref, wr_ref, rb_ref,
                    h1x_ref, grp_ref, rank_ref, cnt_ref):
    mix = (_dot(ya_ref[...], wo_ref[0:ATT_WIDTH, :])
           + _dot(ys_ref[...], wo_ref[ATT_WIDTH:ATT_WIDTH + SG_WIDTH, :])
           + _dot(yc_ref[...], wo_ref[ATT_WIDTH + SG_WIDTH:, :]))
    h1 = _layer_norm(DEEPNORM_ALPHA * h_ref[...] + mix, g_ref[...], b_ref[...])
    h1x_ref[:, 0:D_MODEL] = h1

    w_hi, w_lo = _split_bf16(wr_ref[...])
    h_hi, h_lo = _split_bf16(h1)
    both = _dot_nt(jnp.concatenate([w_hi, w_lo], axis=0), h_hi)
    logits = both[0:N_EXPERTS] + both[N_EXPERTS:] + _dot_nt(w_hi, h_lo)
    scores = jax.nn.sigmoid(logits)
    biased = scores + rb_ref[...]
    sc = [scores[e:e + 1, :] for e in range(N_EXPERTS)]
    bi = [biased[e:e + 1, :] for e in range(N_EXPERTS)]

    best = None
    for g in range(N_EXPERT_GROUPS):
        a0, a1, a2, a3 = bi[4 * g:4 * g + 4]
        hi01, lo01 = jnp.maximum(a0, a1), jnp.minimum(a0, a1)
        hi23, lo23 = jnp.maximum(a2, a3), jnp.minimum(a2, a3)
        top1 = jnp.maximum(hi01, hi23)
        top2 = jnp.maximum(jnp.minimum(hi01, hi23), jnp.maximum(lo01, lo23))
        gs = top1 + top2
        if best is None:
            best, grp = gs, jnp.zeros_like(gs, dtype=jnp.int32)
        else:
            upd = gs > best
            best = jnp.where(upd, gs, best)
            grp = jnp.where(upd, g, grp)

    def in_group(vals, k):
        out = vals[k]
        for g in range(1, N_EXPERT_GROUPS):
            out = jnp.where(grp == g, vals[4 * g + k], out)
        return out

    vb = [in_group(bi, k) for k in range(EXPERTS_PER_GROUP)]
    vs = [in_group(sc, k) for k in range(EXPERTS_PER_GROUP)]
    m1, i1, s1 = vb[0], jnp.zeros_like(grp), vs[0]
    for k in range(1, EXPERTS_PER_GROUP):
        upd = vb[k] > m1
        m1 = jnp.where(upd, vb[k], m1)
        i1 = jnp.where(upd, k, i1)
        s1 = jnp.where(upd, vs[k], s1)
    m2 = jnp.full_like(m1, -jnp.inf)
    i2 = jnp.zeros_like(grp)
    s2 = jnp.zeros_like(s1)
    for k in range(EXPERTS_PER_GROUP):
        upd = (i1 != k) & (vb[k] > m2)
        m2 = jnp.where(upd, vb[k], m2)
        i2 = jnp.where(upd, k, i2)
        s2 = jnp.where(upd, vs[k], s2)
    tot = s1 + s2
    row = lax.broadcasted_iota(jnp.int32, (GATE_ROWS, ROW_TILE), 0)
    rest = (jnp.where(row % EXPERTS_PER_GROUP == i1, s1 / tot, 0.0)
            + jnp.where(row % EXPERTS_PER_GROUP == i2, s2 / tot, 0.0))
    pieces = jnp.zeros((GATE_ROWS, ROW_TILE), F32)
    for i in range(GATE_PIECES):
        piece = rest.astype(BF16).astype(F32)
        pieces = jnp.where(row // EXPERTS_PER_GROUP == i, piece, pieces)
        rest = rest - piece
    t_r = lax.broadcasted_iota(jnp.int32, (ROW_TILE, ROW_TILE), 0)
    t_c = lax.broadcasted_iota(jnp.int32, (ROW_TILE, ROW_TILE), 1)
    pieces_t = _dot_nt(jnp.where(t_r == t_c, 1.0, 0.0).astype(BF16), pieces.astype(BF16))
    p_r = lax.broadcasted_iota(jnp.int32, (GATE_ROWS, LANES), 0)
    p_c = lax.broadcasted_iota(jnp.int32, (GATE_ROWS, LANES), 1)
    add_pieces = jnp.where((p_r % EXPERTS_PER_GROUP == p_c) & (p_r < GATE_PIECES * EXPERTS_PER_GROUP), 1.0, 0.0)
    h1x_ref[:, D_MODEL:] = _dot(pieces_t.astype(BF16), add_pieces.astype(BF16))
    grp_ref[...] = grp
    k_idx = lax.broadcasted_iota(jnp.int32, (EXPERTS_PER_GROUP, ROW_TILE), 0)

    @pl.when(pl.program_id(0) == 0)
    def _():
        cnt_ref[...] = jnp.zeros_like(cnt_ref)

    member = (k_idx == grp).astype(F32)
    earlier = _dot(member.astype(BF16), (t_r < t_c).astype(BF16)) + cnt_ref[...]
    rank_ref[...] = jnp.sum(member * earlier, axis=0, keepdims=True).astype(jnp.int32)
    cnt_ref[...] += jnp.sum(member, axis=1, keepdims=True)


def _outproj(h, y_att, y_sg, y_cv, w_out, ln_g, ln_b, w_router_t, router_bias):
    row = lambda w: pl.BlockSpec((ROW_TILE, w), lambda i: (i, 0))
    full = lambda shape: pl.BlockSpec(shape, lambda i: (0,) * len(shape))
    lane_row = lambda n: pl.BlockSpec((n, ROW_TILE), lambda i: (0, i))
    return pl.pallas_call(
        _outproj_kernel,
        grid=(SEQ // ROW_TILE,),
        in_specs=[row(D_MODEL), row(ATT_WIDTH), row(SG_WIDTH), row(CONV_WIDTH), full((D_MODEL, D_MODEL)),
                  full((1, D_MODEL)), full((1, D_MODEL)), full((N_EXPERTS, D_MODEL)), full((N_EXPERTS, 1))],
        out_specs=[row(MOE_ROW_WIDTH), lane_row(1), lane_row(1)],
        out_shape=[jax.ShapeDtypeStruct((SEQ, MOE_ROW_WIDTH), F32),
                   jax.ShapeDtypeStruct((1, SEQ), jnp.int32),
                   jax.ShapeDtypeStruct((1, SEQ), jnp.int32)],
        scratch_shapes=[pltpu.VMEM((N_EXPERT_GROUPS, ROW_TILE), F32)],
        compiler_params=pltpu.CompilerParams(dimension_semantics=("arbitrary",), vmem_limit_bytes=VMEM_LIMIT),
        name="outproj",
    )(h, y_att, y_sg, y_cv, w_out, ln_g, ln_b, w_router_t, router_bias)


def _moe_plan(grp, rank):
    member = (grp[:, None] == jnp.arange(N_EXPERT_GROUPS)[None, :]).astype(jnp.int32)
    tiles = (member.sum(axis=0) + MOE_TILE - 1) // MOE_TILE
    tile0 = jnp.cumsum(tiles) - tiles
    pos = tile0[grp] * MOE_TILE + rank
    src = jnp.zeros((MOE_SORTED_ROWS,), jnp.int32).at[pos].set(jnp.arange(SEQ, dtype=jnp.int32))
    n_tiles = tiles.sum()
    tile_idx = jnp.minimum(jnp.arange(MOE_MAX_TILES), n_tiles - 1)
    tile_grp = (tile_idx[:, None] >= tile0[None, 1:]).sum(axis=1)
    return pos, src, tile_grp.astype(jnp.int32), n_tiles.reshape(1).astype(jnp.int32)


def _row_copy(src_hbm, buf, sem, slot, src_row, dst_row, n_rows):
    return pltpu.make_async_copy(src_hbm.at[pl.ds(src_row, n_rows), :], buf.at[slot, pl.ds(dst_row, n_rows), :],
                                 sem.at[slot])


def _start_row_gather(idx_ref, src_hbm, buf, sem, tile, slot, n_rows):
    def body(r, carry):
        _row_copy(src_hbm, buf, sem, slot, idx_ref[tile * n_rows + r], r, 1).start()
        return carry
    lax.fori_loop(0, n_rows, body, 0, unroll=8)


def _moe_ffn_kernel(src_ref, grp_ref, nt_ref, hx_hbm, wg_ref, wu_ref, wd_ref, y_ref, buf, sem):
    i = pl.program_id(0)
    n_tiles = nt_ref[0]
    slot = i & 1

    @pl.when(i == 0)
    def _():
        _start_row_gather(src_ref, hx_hbm, buf, sem, 0, 0, MOE_TILE)

    @pl.when(i + 1 < n_tiles)
    def _():
        _start_row_gather(src_ref, hx_hbm, buf, sem, i + 1, 1 - slot, MOE_TILE)

    @pl.when(i < n_tiles)
    def _():
        _row_copy(hx_hbm, buf, sem, slot, 0, 0, MOE_TILE).wait()
        x = buf[slot, :, 0:D_MODEL].astype(BF16)
        w = buf[slot, :, D_MODEL:]
        y = jnp.zeros((MOE_TILE, D_MODEL), F32)
        for k in range(EXPERTS_PER_GROUP):
            a = jax.nn.silu(_dot(x, wg_ref[0, k])) * _dot(x, wu_ref[0, k])
            y = y + w[:, k:k + 1] * _dot(a.astype(BF16), wd_ref[0, k])
        y_ref[...] = y

    @pl.when(i >= n_tiles)
    def _():
        y_ref[...] = jnp.zeros_like(y_ref)


def _moe_ffn(layer, src, tile_grp, n_tiles, hx, w_gate, w_up, w_down):
    experts = lambda a, b: pl.BlockSpec((1, EXPERTS_PER_GROUP, a, b),
                                        lambda i, src, grp, nt: (layer, grp[i], 0, 0))
    return pl.pallas_call(
        _moe_ffn_kernel,
        grid_spec=pltpu.PrefetchScalarGridSpec(
            num_scalar_prefetch=3,
            grid=(MOE_MAX_TILES,),
            in_specs=[pl.BlockSpec(memory_space=pl.ANY),
                      experts(D_MODEL, D_EXPERT), experts(D_MODEL, D_EXPERT), experts(D_EXPERT, D_MODEL)],
            out_specs=pl.BlockSpec((MOE_TILE, D_MODEL), lambda i, src, grp, nt: (i, 0)),
            scratch_shapes=[pltpu.VMEM((2, MOE_TILE, MOE_ROW_WIDTH), F32), pltpu.SemaphoreType.DMA((2,))]),
        out_shape=jax.ShapeDtypeStruct((MOE_SORTED_ROWS, D_MODEL), F32),
        compiler_params=pltpu.CompilerParams(dimension_semantics=("arbitrary",), vmem_limit_bytes=VMEM_LIMIT),
        name="moe_ffn",
    )(src, tile_grp, n_tiles, hx, w_gate, w_up, w_down)


def _moe_combine_kernel(pos_ref, y_hbm, h_ref, g_ref, b_ref, o_ref, buf, sem):
    i = pl.program_id(0)
    slot = i & 1

    @pl.when(i == 0)
    def _():
        _start_row_gather(pos_ref, y_hbm, buf, sem, 0, 0, MOE_SRC_TILE)

    @pl.when(i + 1 < pl.num_programs(0))
    def _():
        _start_row_gather(pos_ref, y_hbm, buf, sem, i + 1, 1 - slot, MOE_SRC_TILE)

    _row_copy(y_hbm, buf, sem, slot, 0, 0, MOE_SRC_TILE).wait()
    o_ref[...] = _layer_norm(DEEPNORM_ALPHA * h_ref[...] + buf[slot], g_ref[...], b_ref[...])


def _moe_combine(y_sorted, pos, h, ln_g, ln_b):
    src_rows = lambda w: pl.BlockSpec((MOE_SRC_TILE, w), lambda i, pos: (i, 0))
    const = lambda shape: pl.BlockSpec(shape, lambda i, pos: (0,) * len(shape))
    return pl.pallas_call(
        _moe_combine_kernel,
        grid_spec=pltpu.PrefetchScalarGridSpec(
            num_scalar_prefetch=1,
            grid=(SEQ // MOE_SRC_TILE,),
            in_specs=[pl.BlockSpec(memory_space=pl.ANY), src_rows(D_MODEL), const((1, D_MODEL)),
                      const((1, D_MODEL))],
            out_specs=src_rows(D_MODEL),
            scratch_shapes=[pltpu.VMEM((2, MOE_SRC_TILE, D_MODEL), F32), pltpu.SemaphoreType.DMA((2,))]),
        out_shape=jax.ShapeDtypeStruct((SEQ, D_MODEL), F32),
        compiler_params=pltpu.CompilerParams(dimension_semantics=("arbitrary",), vmem_limit_bytes=VMEM_LIMIT),
        name="moe_combine",
    )(pos, y_sorted, h, ln_g, ln_b)


def _rope_tables():
    half = HEAD_DIM // 2
    inv_freq = ROPE_THETA ** (-jnp.arange(half, dtype=F32) / half)
    ang = jnp.arange(SEQ, dtype=F32)[:, None] * inv_freq[None, :]
    cos, sin = jnp.cos(ang), jnp.sin(ang)
    reps = LANES // HEAD_DIM
    return jnp.tile(jnp.concatenate([cos, cos], axis=1), (1, reps)), jnp.tile(jnp.concatenate([-sin, sin], axis=1), (1, reps))


def kernel(x, ln_in_g, ln_in_b, w_in, w_out, sg_ln_g, sg_ln_b, sg_w, sg_b, conv_w, conv_b, conv_ln_g, conv_ln_b, ln_mix_g, ln_mix_b, w_router, router_bias, w_gate, w_up, w_down, ln_ffn_g, ln_ffn_b):
    assert x.shape == (1, SEQ, D_MODEL) and w_in.shape == (DEPTH, D_MODEL, IN_COLS)
    cos, sin = _rope_tables()
    row = lambda a: a.reshape(1, -1)
    w_router_t = w_router.T
    rb = router_bias.reshape(N_EXPERTS, 1)
    w_gate_bf16, w_up_bf16, w_down_bf16 = w_gate.astype(BF16), w_up.astype(BF16), w_down.astype(BF16)
    h = x.reshape(SEQ, D_MODEL)
    for l in range(DEPTH):
        sg_bias = jnp.repeat(sg_b[l].T, SG_GROUP_DIM, axis=1)
        w_p = jnp.concatenate([w_in[l][:, :V_OFF], w_in[l][:, SGU_OFF:]], axis=1).astype(BF16)
        w_vt = w_in[l][:, V_OFF:SGU_OFF].T.astype(BF16)
        args = (w_p, w_vt, cos, sin, row(sg_ln_g[l]), row(sg_ln_b[l]), sg_w[l], sg_bias,
                conv_w[l], row(conv_b[l]), row(conv_ln_g[l]), row(conv_ln_b[l]))
        if l == 0:
            h, q, kx, vt, kmean, y_sg, y_cv = _front(True, h, row(ln_in_g), row(ln_in_b), *args)
        else:
            q, kx, vt, kmean, y_sg, y_cv = _front(False, h, None, None, *args)
        y_att = _attn(q, kx, vt, kmean[:, 0, :])
        hx, grp, rank = _outproj(h, y_att, y_sg, y_cv, w_out[l].astype(BF16), row(ln_mix_g[l]),
                                 row(ln_mix_b[l]), w_router_t, rb)
        pos, src, tile_grp, n_tiles = _moe_plan(grp[0], rank[0])
        y_sorted = _moe_ffn(l, src, tile_grp, n_tiles, hx, w_gate_bf16, w_up_bf16, w_down_bf16)
        h = _moe_combine(y_sorted, pos, hx, row(ln_ffn_g[l]), row(ln_ffn_b[l]))
    return h.reshape(1, SEQ, D_MODEL)
```

```python
import functools
import math

import jax
import jax.numpy as jnp
from jax import lax
from jax.experimental import pallas as pl
from jax.experimental.pallas import tpu as pltpu

D_MODEL = 1024
SEQ = 16384
DEPTH = 2
HEAD_DIM = 64
ATT_WIDTH = 512
ATT_HEADS = 8
SG_WIDTH = 256
SG_GROUPS = 4
SG_GROUP_DIM = 64
CONV_WIDTH = 256
K_OFF = 512
V_OFF = 1024
SGU_OFF = 1536
SGV_OFF = 1792
CA_OFF = 2048
CG_OFF = 2304
IN_COLS = 2560
P_SGU = SGU_OFF - ATT_WIDTH
P_SGV = SGV_OFF - ATT_WIDTH
P_CA = CA_OFF - ATT_WIDTH
P_CG = CG_OFF - ATT_WIDTH
P_COLS = IN_COLS - ATT_WIDTH
MOBA_BLOCK = 256
MOBA_TOPK = 3
N_BLOCKS = SEQ // MOBA_BLOCK
SG_CHUNK = 128
CONV_TAPS = 31
ROPE_THETA = 10000.0
N_EXPERTS = 16
N_EXPERT_GROUPS = 4
EXPERTS_PER_GROUP = 4
D_EXPERT = 512
LN_EPS = 1e-5
NEG = -1e30
DEEPNORM_ALPHA = (2.0 * DEPTH) ** 0.25

LANES = 128
SUBLANES = 8
ROW_TILE = MOBA_BLOCK
CONV_HALO = 32
LOG2_E = math.log2(math.e)
VT_ROWS = HEAD_DIM + 16
ATT_GROUP_SHIFT = 2
ATT_GROUP = 1 << ATT_GROUP_SHIFT
ATT_Q_BLOCKS = 2
ATT_Q_TILE = ATT_Q_BLOCKS * MOBA_BLOCK
ATT_GROUP_KEYS = ATT_GROUP * MOBA_BLOCK
MOE_ROW_WIDTH = D_MODEL + LANES
GATE_PIECES = 3
GATE_ROWS = 16
MOE_TILE = 256
MOE_SRC_TILE = 512
MOE_MAX_TILES = SEQ // MOE_TILE + N_EXPERT_GROUPS
MOE_SORTED_ROWS = MOE_MAX_TILES * MOE_TILE
VMEM_LIMIT = 48 * 1024 * 1024

F32 = jnp.float32
BF16 = jnp.bfloat16


def _dot(a, b):
    return jnp.dot(a, b, preferred_element_type=F32)


def _dot_nt(a, b):
    return lax.dot_general(a, b, (((1,), (1,)), ((), ())), preferred_element_type=F32)


def _layer_norm(x, g, b):
    mu = jnp.mean(x, axis=-1, keepdims=True)
    xc = x - mu
    var = jnp.mean(xc * xc, axis=-1, keepdims=True)
    return xc * lax.rsqrt(var + LN_EPS) * g + b


def _split_bf16(x):
    hi = x.astype(BF16)
    lo = (x - hi.astype(F32)).astype(BF16)
    return hi, lo


def _front_kernel(apply_ln, *refs):
    if apply_ln:
        (x_ref, lng_ref, lnb_ref, w_ref, wvt_ref, cos_ref, sin_ref, sglng_ref, sglnb_ref, sgw_ref, sgb_ref,
         cw_ref, cb_ref, clng_ref, clnb_ref,
         h_ref, q_ref, kx_ref, vt_ref, km_ref, ysg_ref, ycv_ref, ext_ref) = refs
    else:
        (x_ref, w_ref, wvt_ref, cos_ref, sin_ref, sglng_ref, sglnb_ref, sgw_ref, sgb_ref,
         cw_ref, cb_ref, clng_ref, clnb_ref,
         q_ref, kx_ref, vt_ref, km_ref, ysg_ref, ycv_ref, ext_ref) = refs
    i = pl.program_id(0)
    h = x_ref[...]
    if apply_ln:
        h = _layer_norm(h, lng_ref[...], lnb_ref[...])
        h_ref[...] = h
    hb = h.astype(BF16)
    p = _dot(hb, w_ref[...])
    vt_ref[:, 0:HEAD_DIM, :] = _dot_nt(wvt_ref[...], hb).reshape(ATT_HEADS, HEAD_DIM, ROW_TILE).astype(BF16)
    vt_ref[:, HEAD_DIM:, :] = jnp.ones((ATT_HEADS, VT_ROWS - HEAD_DIM, ROW_TILE), BF16)

    lane = lax.broadcasted_iota(jnp.int32, (ROW_TILE, LANES), 1)
    first_half = (lane % HEAD_DIM) < (HEAD_DIM // 2)
    low_head = lane < HEAD_DIM
    cos = cos_ref[...]
    sin = sin_ref[...]

    def rot(x):
        swapped = jnp.where(first_half, pltpu.roll(x, LANES - HEAD_DIM // 2, 1), pltpu.roll(x, HEAD_DIM // 2, 1))
        return x * cos + swapped * sin

    scale = HEAD_DIM ** -0.5 * LOG2_E
    for j in range(ATT_WIDTH // LANES):
        sl = slice(j * LANES, (j + 1) * LANES)
        q_ref[:, sl] = (rot(p[:, sl]) * scale).astype(BF16)
        kr = rot(p[:, K_OFF + j * LANES:K_OFF + (j + 1) * LANES])
        km_ref[0, :, sl] = jnp.broadcast_to(jnp.mean(kr, axis=0, keepdims=True), (SUBLANES, LANES))
        kx_ref[2 * j] = jnp.where(low_head, kr, (lane - HEAD_DIM == i).astype(F32)).astype(BF16)
        kx_ref[2 * j + 1] = jnp.where(low_head, (lane == i).astype(F32), kr).astype(BF16)

    u = jax.nn.gelu(p[:, P_SGU:P_SGV])
    vv = _layer_norm(jax.nn.gelu(p[:, P_SGV:P_CA]), sglng_ref[...], sglnb_ref[...]).astype(BF16)
    r_i = lax.broadcasted_iota(jnp.int32, (SG_CHUNK, SG_CHUNK), 0)
    c_i = lax.broadcasted_iota(jnp.int32, (SG_CHUNK, SG_CHUNK), 1)
    lane_sg = lax.broadcasted_iota(jnp.int32, (SG_CHUNK, SG_WIDTH), 1)
    for c in range(ROW_TILE // SG_CHUNK):
        rows = slice(c * SG_CHUNK, (c + 1) * SG_CHUNK)
        mixed = sgb_ref[...]
        for g in range(SG_GROUPS):
            wg = jnp.where(c_i <= r_i, sgw_ref[g], 0.0).astype(BF16)
            full = _dot(wg, vv[rows])
            mixed = mixed + jnp.where(lane_sg // SG_GROUP_DIM == g, full, 0.0)
        ysg_ref[rows, :] = (u[rows] * mixed).astype(BF16)

    @pl.when(i == 0)
    def _():
        ext_ref[0:CONV_HALO, :] = jnp.zeros((CONV_HALO, CONV_WIDTH), F32)

    ext_ref[CONV_HALO:, :] = p[:, P_CA:P_CG] * jax.nn.sigmoid(p[:, P_CG:P_COLS])
    acc = jnp.zeros((ROW_TILE, CONV_WIDTH), F32) + cb_ref[...]
    first = CONV_HALO - (CONV_TAPS - 1)
    ext = ext_ref[...]
    n_ext = CONV_HALO + ROW_TILE
    for shift in range(SUBLANES):
        shifted = ext if shift == 0 else pltpu.roll(ext, n_ext - shift, 0)
        for o in range(first + (shift - first) % SUBLANES, first + CONV_TAPS, SUBLANES):
            acc = acc + cw_ref[o - first:o - first + 1, :] * shifted[o - shift:o - shift + ROW_TILE]
    ext_ref[0:CONV_HALO, :] = ext_ref[ROW_TILE:ROW_TILE + CONV_HALO, :]
    hc = _layer_norm(acc, clng_ref[...], clnb_ref[...])
    ycv_ref[...] = (hc * jax.nn.sigmoid(hc)).astype(BF16)


def _front(apply_ln, x, ln_g, ln_b, w_p, w_vt, cos, sin, sg_ln_g, sg_ln_b, sg_w, sg_bias, conv_w, conv_b,
           conv_ln_g, conv_ln_b):
    n_tiles = SEQ // ROW_TILE
    row = lambda w: pl.BlockSpec((ROW_TILE, w), lambda i: (i, 0))
    full = lambda shape: pl.BlockSpec(shape, lambda i: (0,) * len(shape))
    in_specs = [row(D_MODEL)]
    args = [x]
    if apply_ln:
        in_specs += [full((1, D_MODEL)), full((1, D_MODEL))]
        args += [ln_g, ln_b]
    in_specs += [full((D_MODEL, P_COLS)), full((ATT_WIDTH, D_MODEL)), row(LANES), row(LANES),
                 full((1, SG_WIDTH)), full((1, SG_WIDTH)),
                 full((SG_GROUPS, SG_CHUNK, SG_CHUNK)), full((SG_CHUNK, SG_WIDTH)),
                 full((CONV_TAPS, CONV_WIDTH)), full((1, CONV_WIDTH)), full((1, CONV_WIDTH)),
                 full((1, CONV_WIDTH))]
    args += [w_p, w_vt, cos, sin, sg_ln_g, sg_ln_b, sg_w, sg_bias, conv_w, conv_b, conv_ln_g, conv_ln_b]
    out_shape = [jax.ShapeDtypeStruct((SEQ, ATT_WIDTH), BF16),
                 jax.ShapeDtypeStruct((ATT_HEADS, SEQ, LANES), BF16),
                 jax.ShapeDtypeStruct((ATT_HEADS, VT_ROWS, SEQ), BF16),
                 jax.ShapeDtypeStruct((N_BLOCKS, SUBLANES, ATT_WIDTH), F32),
                 jax.ShapeDtypeStruct((SEQ, SG_WIDTH), BF16),
                 jax.ShapeDtypeStruct((SEQ, CONV_WIDTH), BF16)]
    out_specs = [row(ATT_WIDTH), pl.BlockSpec((ATT_HEADS, ROW_TILE, LANES), lambda i: (0, i, 0)),
                 pl.BlockSpec((ATT_HEADS, VT_ROWS, ROW_TILE), lambda i: (0, 0, i)),
                 pl.BlockSpec((1, SUBLANES, ATT_WIDTH), lambda i: (i, 0, 0)),
                 row(SG_WIDTH), row(CONV_WIDTH)]
    if apply_ln:
        out_shape = [jax.ShapeDtypeStruct((SEQ, D_MODEL), F32)] + out_shape
        out_specs = [row(D_MODEL)] + out_specs
    return pl.pallas_call(
        functools.partial(_front_kernel, apply_ln),
        grid=(n_tiles,),
        in_specs=in_specs,
        out_specs=out_specs,
        out_shape=out_shape,
        scratch_shapes=[pltpu.VMEM((CONV_HALO + ROW_TILE, CONV_WIDTH), F32)],
        compiler_params=pltpu.CompilerParams(dimension_semantics=("arbitrary",), vmem_limit_bytes=VMEM_LIMIT),
        name="front_ln" if apply_ln else "front",
    )(*args)


def _attn_kernel(q_ref, kx_ref, vt_ref, km_ref, o_ref, s0_ref, s1_ref, mx0_ref, mx1_ref):
    first_blk = pl.program_id(1) * ATT_Q_BLOCKS
    q = q_ref[...]

    km = km_ref[...]
    km_lane = lax.broadcasted_iota(jnp.int32, (N_BLOCKS, LANES), 1)
    km_rows = jnp.concatenate([jnp.where(km_lane >= HEAD_DIM, km, 0.0), jnp.where(km_lane < HEAD_DIM, km, 0.0)],
                              axis=0)
    km_hi, km_lo = _split_bf16(km_rows)
    eye = (lax.broadcasted_iota(jnp.int32, (LANES, LANES), 0)
           == lax.broadcasted_iota(jnp.int32, (LANES, LANES), 1)).astype(F32).astype(BF16)
    res = _dot_nt(jnp.concatenate([km_hi, km_lo, eye], axis=0), q)
    q_t = res[2 * LANES:].astype(BF16)
    gate = (res[0:LANES] + res[LANES:2 * LANES]).reshape(2, N_BLOCKS, ATT_Q_TILE)
    blk = lax.broadcasted_iota(jnp.int32, (2, N_BLOCKS, ATT_Q_TILE), 1)
    own = first_blk + lax.broadcasted_iota(jnp.int32, (2, N_BLOCKS, ATT_Q_TILE), 2) // MOBA_BLOCK
    gate = jnp.where(blk < own, gate, NEG)

    sel = jnp.zeros((2, N_BLOCKS, ATT_Q_TILE), jnp.bool_)
    blk_f = blk.astype(F32)
    for r in range(MOBA_TOPK):
        best = jnp.max(gate, axis=1, keepdims=True)
        first = jnp.min(jnp.where(gate == best, blk_f, float(N_BLOCKS)), axis=1, keepdims=True)
        pick = blk_f == first
        sel = sel | (pick & (own > r))
        gate = jnp.where(pick, -jnp.inf, gate)
    bias_t = jnp.where(sel, 0.0, NEG).astype(BF16).reshape(LANES, ATT_Q_TILE)
    low = lax.broadcasted_iota(jnp.int32, (LANES, ATT_Q_TILE), 0) < HEAD_DIM
    zero = jnp.zeros_like(q_t)
    qx_past = (jnp.where(low, q_t, bias_t), jnp.where(low, bias_t, q_t))
    qx_own = (jnp.where(low, q_t, zero), jnp.where(low, zero, q_t))

    key_i = lax.broadcasted_iota(jnp.int32, (ATT_Q_TILE, ATT_Q_TILE), 0)
    qry_i = lax.broadcasted_iota(jnp.int32, (ATT_Q_TILE, ATT_Q_TILE), 1)
    own_mask = (key_i <= qry_i) & (key_i // MOBA_BLOCK == qry_i // MOBA_BLOCK)
    start_own = pl.multiple_of(first_blk * MOBA_BLOCK, ATT_Q_TILE)

    state = []
    for par in range(2):
        s = _dot(kx_ref[par, pl.ds(start_own, ATT_Q_TILE), :], qx_own[par])
        s = jnp.where(own_mask, s, NEG)
        m = jnp.max(s, axis=0, keepdims=True)
        p = jnp.exp2(s - m)
        acc = _dot(vt_ref[par, :, pl.ds(start_own, ATT_Q_TILE)], p.astype(BF16))
        state += [m, acc]

    n_groups = lax.shift_right_logical(first_blk + (ATT_Q_BLOCKS - 1) + (ATT_GROUP - 1), ATT_GROUP_SHIFT)

    def scores(g, s_ref, mx_ref):
        start = pl.multiple_of(g * ATT_GROUP_KEYS, ATT_GROUP_KEYS)
        for par in range(2):
            s = _dot(kx_ref[par, pl.ds(start, ATT_GROUP_KEYS), :], qx_past[par])
            s_ref[par] = s
            mx_ref[par] = jnp.max(s, axis=0, keepdims=True)

    def consume(g, s_ref, mx_ref, carry):
        start = pl.multiple_of(g * ATT_GROUP_KEYS, ATT_GROUP_KEYS)
        out = []
        for par in range(2):
            m, acc = carry[2 * par:2 * par + 2]
            m_new = jnp.maximum(m, mx_ref[par])
            a = jnp.exp2(m - m_new)
            p = jnp.exp2(s_ref[par] - m_new)
            acc = a * acc + _dot(vt_ref[par, :, pl.ds(start, ATT_GROUP_KEYS)], p.astype(BF16))
            out += [m_new, acc]
        return tuple(out)

    scores(0, s0_ref, mx0_ref)

    def body(g, carry):
        def even(c):
            scores(g + 1, s1_ref, mx1_ref)
            return consume(g, s0_ref, mx0_ref, c)

        def odd(c):
            scores(g + 1, s0_ref, mx0_ref)
            return consume(g, s1_ref, mx1_ref, c)

        return lax.cond((g & 1) == 0, even, odd, carry)

    last = n_groups - 1
    state = lax.fori_loop(0, last, body, tuple(state))
    state = lax.cond((last & 1) == 0, lambda c: consume(last, s0_ref, mx0_ref, c),
                     lambda c: consume(last, s1_ref, mx1_ref, c), state)
    o_t = jnp.concatenate([acc[0:HEAD_DIM] / acc[HEAD_DIM:HEAD_DIM + 1] for acc in (state[1], state[3])],
                          axis=0)
    o_ref[...] = o_t.T.astype(BF16)


def _attn(q, kx, vt, kmean):
    n_pairs = ATT_WIDTH // LANES
    return pl.pallas_call(
        _attn_kernel,
        grid=(n_pairs, SEQ // ATT_Q_TILE),
        in_specs=[pl.BlockSpec((ATT_Q_TILE, LANES), lambda j, t: (t, j)),
                  pl.BlockSpec((2, SEQ, LANES), lambda j, t: (j, 0, 0)),
                  pl.BlockSpec((2, VT_ROWS, SEQ), lambda j, t: (j, 0, 0)),
                  pl.BlockSpec((N_BLOCKS, LANES), lambda j, t: (0, j))],
        out_specs=pl.BlockSpec((ATT_Q_TILE, LANES), lambda j, t: (t, j)),
        out_shape=jax.ShapeDtypeStruct((SEQ, ATT_WIDTH), BF16),
        scratch_shapes=([pltpu.VMEM((2, ATT_GROUP_KEYS, ATT_Q_TILE), F32)] * 2
                        + [pltpu.VMEM((2, 1, ATT_Q_TILE), F32)] * 2),
        compiler_params=pltpu.CompilerParams(dimension_semantics=("arbitrary", "arbitrary"),
                                             vmem_limit_bytes=VMEM_LIMIT),
        name="attn",
    )(q, kx, vt, kmean)


def _outproj_kernel(h_ref, ya_ref, ys_ref, yc_ref, wo_ref, g_ref, b_ref, wr_ref, rb_ref,
                    h1x_ref, grp_ref, rank_ref, cnt_ref):
    mix = (_dot(ya_ref[...], wo_ref[0:ATT_WIDTH, :])
           + _dot(ys_ref[...], wo_ref[ATT_WIDTH:ATT_WIDTH + SG_WIDTH, :])
           + _dot(yc_ref[...], wo_ref[ATT_WIDTH + SG_WIDTH:, :]))
    h1 = _layer_norm(DEEPNORM_ALPHA * h_ref[...] + mix, g_ref[...], b_ref[...])
    h1x_ref[:, 0:D_MODEL] = h1

    w_hi, w_lo = _split_bf16(wr_ref[...])
    h_hi, h_lo = _split_bf16(h1)
    both = _dot_nt(jnp.concatenate([w_hi, w_lo], axis=0), h_hi)
    logits = both[0:N_EXPERTS] + both[N_EXPERTS:] + _dot_nt(w_hi, h_lo)
    scores = jax.nn.sigmoid(logits)
    biased = scores + rb_ref[...]
    sc = [scores[e:e + 1, :] for e in range(N_EXPERTS)]
    bi = [biased[e:e + 1, :] for e in range(N_EXPERTS)]

    best = None
    for g in range(N_EXPERT_GROUPS):
        a0, a1, a2, a3 = bi[4 * g:4 * g + 4]
        hi01, lo01 = jnp.maximum(a0, a1), jnp.minimum(a0, a1)
        hi23, lo23 = jnp.maximum(a2, a3), jnp.minimum(a2, a3)
        top1 = jnp.maximum(hi01, hi23)
        top2 = jnp.maximum(jnp.minimum(hi01, hi23), jnp.maximum(lo01, lo23))
        gs = top1 + top2
        if best is None:
            best, grp = gs, jnp.zeros_like(gs, dtype=jnp.int32)
        else:
            upd = gs > best
            best = jnp.where(upd, gs, best)
            grp = jnp.where(upd, g, grp)

    def in_group(vals, k):
        out = vals[k]
        for g in range(1, N_EXPERT_GROUPS):
            out = jnp.where(grp == g, vals[4 * g + k], out)
        return out

    vb = [in_group(bi, k) for k in range(EXPERTS_PER_GROUP)]
    vs = [in_group(sc, k) for k in range(EXPERTS_PER_GROUP)]
    m1, i1, s1 = vb[0], jnp.zeros_like(grp), vs[0]
    for k in range(1, EXPERTS_PER_GROUP):
        upd = vb[k] > m1
        m1 = jnp.where(upd, vb[k], m1)
        i1 = jnp.where(upd, k, i1)
        s1 = jnp.where(upd, vs[k], s1)
    m2 = jnp.full_like(m1, -jnp.inf)
    i2 = jnp.zeros_like(grp)
    s2 = jnp.zeros_like(s1)
    for k in range(EXPERTS_PER_GROUP):
        upd = (i1 != k) & (vb[k] > m2)
        m2 = jnp.where(upd, vb[k], m2)
        i2 = jnp.where(upd, k, i2)
        s2 = jnp.where(upd, vs[k], s2)
    tot = s1 + s2
    row = lax.broadcasted_iota(jnp.int32, (GATE_ROWS, ROW_TILE), 0)
    rest = (jnp.where(row % EXPERTS_PER_GROUP == i1, s1 / tot, 0.0)
            + jnp.where(row % EXPERTS_PER_GROUP == i2, s2 / tot, 0.0))
    pieces = jnp.zeros((GATE_ROWS, ROW_TILE), F32)
    for i in range(GATE_PIECES):
        piece = rest.astype(BF16).astype(F32)
        pieces = jnp.where(row // EXPERTS_PER_GROUP == i, piece, pieces)
        rest = rest - piece
    t_r = lax.broadcasted_iota(jnp.int32, (ROW_TILE, ROW_TILE), 0)
    t_c = lax.broadcasted_iota(jnp.int32, (ROW_TILE, ROW_TILE), 1)
    pieces_t = _dot_nt(jnp.where(t_r == t_c, 1.0, 0.0).astype(BF16), pieces.astype(BF16))
    p_r = lax.broadcasted_iota(jnp.int32, (GATE_ROWS, LANES), 0)
    p_c = lax.broadcasted_iota(jnp.int32, (GATE_ROWS, LANES), 1)
    add_pieces = jnp.where((p_r % EXPERTS_PER_GROUP == p_c) & (p_r < GATE_PIECES * EXPERTS_PER_GROUP), 1.0, 0.0)
    h1x_ref[:, D_MODEL:] = _dot(pieces_t.astype(BF16), add_pieces.astype(BF16))
    grp_ref[...] = grp
    k_idx = lax.broadcasted_iota(jnp.int32, (EXPERTS_PER_GROUP, ROW_TILE), 0)

    @pl.when(pl.program_id(0) == 0)
    def _():
        cnt_ref[...] = jnp.zeros_like(cnt_ref)

    member = (k_idx == grp).astype(F32)
    earlier = _dot(member.astype(BF16), (t_r < t_c).astype(BF16)) + cnt_ref[...]
    rank_ref[...] = jnp.sum(member * earlier, axis=0, keepdims=True).astype(jnp.int32)
    cnt_ref[...] += jnp.sum(member, axis=1, keepdims=True)


def _outproj(h, y_att, y_sg, y_cv, w_out, ln_g, ln_b, w_router_t, router_bias):
    row = lambda w: pl.BlockSpec((ROW_TILE, w), lambda i: (i, 0))
    full = lambda shape: pl.BlockSpec(shape, lambda i: (0,) * len(shape))
    lane_row = lambda n: pl.BlockSpec((n, ROW_TILE), lambda i: (0, i))
    return pl.pallas_call(
        _outproj_kernel,
        grid=(SEQ // ROW_TILE,),
        in_specs=[row(D_MODEL), row(ATT_WIDTH), row(SG_WIDTH), row(CONV_WIDTH), full((D_MODEL, D_MODEL)),
                  full((1, D_MODEL)), full((1, D_MODEL)), full((N_EXPERTS, D_MODEL)), full((N_EXPERTS, 1))],
        out_specs=[row(MOE_ROW_WIDTH), lane_row(1), lane_row(1)],
        out_shape=[jax.ShapeDtypeStruct((SEQ, MOE_ROW_WIDTH), F32),
                   jax.ShapeDtypeStruct((1, SEQ), jnp.int32),
                   jax.ShapeDtypeStruct((1, SEQ), jnp.int32)],
        scratch_shapes=[pltpu.VMEM((N_EXPERT_GROUPS, ROW_TILE), F32)],
        compiler_params=pltpu.CompilerParams(dimension_semantics=("arbitrary",), vmem_limit_bytes=VMEM_LIMIT),
        name="outproj",
    )(h, y_att, y_sg, y_cv, w_out, ln_g, ln_b, w_router_t, router_bias)


def _moe_plan(grp, rank):
    member = (grp[:, None] == jnp.arange(N_EXPERT_GROUPS)[None, :]).astype(jnp.int32)
    tiles = (member.sum(axis=0) + MOE_TILE - 1) // MOE_TILE
    tile0 = jnp.cumsum(tiles) - tiles
    pos = tile0[grp] * MOE_TILE + rank
    src = jnp.zeros((MOE_SORTED_ROWS,), jnp.int32).at[pos].set(jnp.arange(SEQ, dtype=jnp.int32))
    n_tiles = tiles.sum()
    tile_idx = jnp.minimum(jnp.arange(MOE_MAX_TILES), n_tiles - 1)
    tile_grp = (tile_idx[:, None] >= tile0[None, 1:]).sum(axis=1)
    return pos, src, tile_grp.astype(jnp.int32), n_tiles.reshape(1).astype(jnp.int32)


def _row_copy(src_hbm, buf, sem, slot, src_row, dst_row, n_rows):
    return pltpu.make_async_copy(src_hbm.at[pl.ds(src_row, n_rows), :], buf.at[slot, pl.ds(dst_row, n_rows), :],
                                 sem.at[slot])


def _start_row_gather(idx_ref, src_hbm, buf, sem, tile, slot, n_rows):
    def body(r, carry):
        _row_copy(src_hbm, buf, sem, slot, idx_ref[tile * n_rows + r], r, 1).start()
        return carry
    lax.fori_loop(0, n_rows, body, 0, unroll=8)


def _moe_ffn_kernel(src_ref, grp_ref, nt_ref, hx_hbm, wg_ref, wu_ref, wd_ref, y_ref, buf, sem):
    i = pl.program_id(0)
    n_tiles = nt_ref[0]
    slot = i & 1

    @pl.when(i == 0)
    def _():
        _start_row_gather(src_ref, hx_hbm, buf, sem, 0, 0, MOE_TILE)

    @pl.when(i + 1 < n_tiles)
    def _():
        _start_row_gather(src_ref, hx_hbm, buf, sem, i + 1, 1 - slot, MOE_TILE)

    @pl.when(i < n_tiles)
    def _():
        _row_copy(hx_hbm, buf, sem, slot, 0, 0, MOE_TILE).wait()
        x = buf[slot, :, 0:D_MODEL].astype(BF16)
        w = buf[slot, :, D_MODEL:]
        y = jnp.zeros((MOE_TILE, D_MODEL), F32)
        for k in range(EXPERTS_PER_GROUP):
            a = jax.nn.silu(_dot(x, wg_ref[0, k])) * _dot(x, wu_ref[0, k])
            y = y + w[:, k:k + 1] * _dot(a.astype(BF16), wd_ref[0, k])
        y_ref[...] = y

    @pl.when(i >= n_tiles)
    def _():
        y_ref[...] = jnp.zeros_like(y_ref)


def _moe_ffn(layer, src, tile_grp, n_tiles, hx, w_gate, w_up, w_down):
    experts = lambda a, b: pl.BlockSpec((1, EXPERTS_PER_GROUP, a, b),
                                        lambda i, src, grp, nt: (layer, grp[i], 0, 0))
    return pl.pallas_call(
        _moe_ffn_kernel,
        grid_spec=pltpu.PrefetchScalarGridSpec(
            num_scalar_prefetch=3,
            grid=(MOE_MAX_TILES,),
            in_specs=[pl.BlockSpec(memory_space=pl.ANY),
                      experts(D_MODEL, D_EXPERT), experts(D_MODEL, D_EXPERT), experts(D_EXPERT, D_MODEL)],
            out_specs=pl.BlockSpec((MOE_TILE, D_MODEL), lambda i, src, grp, nt: (i, 0)),
            scratch_shapes=[pltpu.VMEM((2, MOE_TILE, MOE_ROW_WIDTH), F32), pltpu.SemaphoreType.DMA((2,))]),
        out_shape=jax.ShapeDtypeStruct((MOE_SORTED_ROWS, D_MODEL), F32),
        compiler_params=pltpu.CompilerParams(dimension_semantics=("arbitrary",), vmem_limit_bytes=VMEM_LIMIT),
        name="moe_ffn",
    )(src, tile_grp, n_tiles, hx, w_gate, w_up, w_down)


def _moe_combine_kernel(pos_ref, y_hbm, h_ref, g_ref, b_ref, o_ref, buf, sem):
    i = pl.program_id(0)
    slot = i & 1

    @pl.when(i == 0)
    def _():
        _start_row_gather(pos_ref, y_hbm, buf, sem, 0, 0, MOE_SRC_TILE)

    @pl.when(i + 1 < pl.num_programs(0))
    def _():
        _start_row_gather(pos_ref, y_hbm, buf, sem, i + 1, 1 - slot, MOE_SRC_TILE)

    _row_copy(y_hbm, buf, sem, slot, 0, 0, MOE_SRC_TILE).wait()
    o_ref[...] = _layer_norm(DEEPNORM_ALPHA * h_ref[...] + buf[slot], g_ref[...], b_ref[...])


def _moe_combine(y_sorted, pos, h, ln_g, ln_b):
    src_rows = lambda w: pl.BlockSpec((MOE_SRC_TILE, w), lambda i, pos: (i, 0))
    const = lambda shape: pl.BlockSpec(shape, lambda i, pos: (0,) * len(shape))
    return pl.pallas_call(
        _moe_combine_kernel,
        grid_spec=pltpu.PrefetchScalarGridSpec(
            num_scalar_prefetch=1,
            grid=(SEQ // MOE_SRC_TILE,),
            in_specs=[pl.BlockSpec(memory_space=pl.ANY), src_rows(D_MODEL), const((1, D_MODEL)),
                      const((1, D_MODEL))],
            out_specs=src_rows(D_MODEL),
            scratch_shapes=[pltpu.VMEM((2, MOE_SRC_TILE, D_MODEL), F32), pltpu.SemaphoreType.DMA((2,))]),
        out_shape=jax.ShapeDtypeStruct((SEQ, D_MODEL), F32),
        compiler_params=pltpu.CompilerParams(dimension_semantics=("arbitrary",), vmem_limit_bytes=VMEM_LIMIT),
        name="moe_combine",
    )(pos, y_sorted, h, ln_g, ln_b)


def _rope_tables():
    half = HEAD_DIM // 2
    inv_freq = ROPE_THETA ** (-jnp.arange(half, dtype=F32) / half)
    ang = jnp.arange(SEQ, dtype=F32)[:, None] * inv_freq[None, :]
    cos, sin = jnp.cos(ang), jnp.sin(ang)
    reps = LANES // HEAD_DIM
    return jnp.tile(jnp.concatenate([cos, cos], axis=1), (1, reps)), jnp.tile(jnp.concatenate([-sin, sin], axis=1), (1, reps))


def kernel(x, ln_in_g, ln_in_b, w_in, w_out, sg_ln_g, sg_ln_b, sg_w, sg_b, conv_w, conv_b, conv_ln_g, conv_ln_b, ln_mix_g, ln_mix_b, w_router, router_bias, w_gate, w_up, w_down, ln_ffn_g, ln_ffn_b):
    assert x.shape == (1, SEQ, D_MODEL) and w_in.shape == (DEPTH, D_MODEL, IN_COLS)
    cos, sin = _rope_tables()
    row = lambda a: a.reshape(1, -1)
    w_router_t = w_router.T
    rb = router_bias.reshape(N_EXPERTS, 1)
    w_gate_bf16, w_up_bf16, w_down_bf16 = w_gate.astype(BF16), w_up.astype(BF16), w_down.astype(BF16)
    h = x.reshape(SEQ, D_MODEL)
    for l in range(DEPTH):
        sg_bias = jnp.repeat(sg_b[l].T, SG_GROUP_DIM, axis=1)
        w_p = jnp.concatenate([w_in[l][:, :V_OFF], w_in[l][:, SGU_OFF:]], axis=1).astype(BF16)
        w_vt = w_in[l][:, V_OFF:SGU_OFF].T.astype(BF16)
        args = (w_p, w_vt, cos, sin, row(sg_ln_g[l]), row(sg_ln_b[l]), sg_w[l], sg_bias,
                conv_w[l], row(conv_b[l]), row(conv_ln_g[l]), row(conv_ln_b[l]))
        if l == 0:
            h, q, kx, vt, kmean, y_sg, y_cv = _front(True, h, row(ln_in_g), row(ln_in_b), *args)
        else:
            q, kx, vt, kmean, y_sg, y_cv = _front(False, h, None, None, *args)
        y_att = _attn(q, kx, vt, kmean[:, 0, :])
        hx, grp, rank = _outproj(h, y_att, y_sg, y_cv, w_out[l].astype(BF16), row(ln_mix_g[l]),
                                 row(ln_mix_b[l]), w_router_t, rb)
        pos, src, tile_grp, n_tiles = _moe_plan(grp[0], rank[0])
        y_sorted = _moe_ffn(l, src, tile_grp, n_tiles, hx, w_gate_bf16, w_up_bf16, w_down_bf16)
        h = _moe_combine(y_sorted, pos, hx, row(ln_ffn_g[l]), row(ln_ffn_b[l]))
    return h.reshape(1, SEQ, D_MODEL)
```

```python
import functools
import math

import jax
import jax.numpy as jnp
from jax import lax
from jax.experimental import pallas as pl
from jax.experimental.pallas import tpu as pltpu

D_MODEL = 1024
SEQ = 16384
DEPTH = 2
HEAD_DIM = 64
ATT_WIDTH = 512
ATT_HEADS = 8
SG_WIDTH = 256
SG_GROUPS = 4
SG_GROUP_DIM = 64
CONV_WIDTH = 256
K_OFF = 512
V_OFF = 1024
SGU_OFF = 1536
SGV_OFF = 1792
CA_OFF = 2048
CG_OFF = 2304
IN_COLS = 2560
P_SGU = SGU_OFF - ATT_WIDTH
P_SGV = SGV_OFF - ATT_WIDTH
P_CA = CA_OFF - ATT_WIDTH
P_CG = CG_OFF - ATT_WIDTH
P_COLS = IN_COLS - ATT_WIDTH
MOBA_BLOCK = 256
MOBA_TOPK = 3
N_BLOCKS = SEQ // MOBA_BLOCK
SG_CHUNK = 128
CONV_TAPS = 31
ROPE_THETA = 10000.0
N_EXPERTS = 16
N_EXPERT_GROUPS = 4
EXPERTS_PER_GROUP = 4
D_EXPERT = 512
LN_EPS = 1e-5
NEG = -1e30
DEEPNORM_ALPHA = (2.0 * DEPTH) ** 0.25

LANES = 128
SUBLANES = 8
ROW_TILE = MOBA_BLOCK
CONV_HALO = 32
LOG2_E = math.log2(math.e)
VT_ROWS = HEAD_DIM + 16
ATT_GROUP_SHIFT = 2
ATT_GROUP = 1 << ATT_GROUP_SHIFT
ATT_Q_BLOCKS = 2
ATT_Q_TILE = ATT_Q_BLOCKS * MOBA_BLOCK
ATT_GROUP_KEYS = ATT_GROUP * MOBA_BLOCK
MOE_ROW_WIDTH = D_MODEL + LANES
GATE_PIECES = 3
GATE_ROWS = 16
MOE_TILE = 256
MOE_SRC_TILE = 512
MOE_MAX_TILES = SEQ // MOE_TILE + N_EXPERT_GROUPS
MOE_SORTED_ROWS = MOE_MAX_TILES * MOE_TILE
VMEM_LIMIT = 48 * 1024 * 1024

F32 = jnp.float32
BF16 = jnp.bfloat16


def _dot(a, b):
    return jnp.dot(a, b, preferred_element_type=F32)


def _dot_nt(a, b):
    return lax.dot_general(a, b, (((1,), (1,)), ((), ())), preferred_element_type=F32)


def _layer_norm(x, g, b):
    mu = jnp.mean(x, axis=-1, keepdims=True)
    xc = x - mu
    var = jnp.mean(xc * xc, axis=-1, keepdims=True)
    return xc * lax.rsqrt(var + LN_EPS) * g + b


def _split_bf16(x):
    hi = x.astype(BF16)
    lo = (x - hi.astype(F32)).astype(BF16)
    return hi, lo


def _front_kernel(apply_ln, *refs):
    if apply_ln:
        (x_ref, lng_ref, lnb_ref, w_ref, wvt_ref, cos_ref, sin_ref, sglng_ref, sglnb_ref, sgw_ref, sgb_ref,
         cw_ref, cb_ref, clng_ref, clnb_ref,
         h_ref, q_ref, kx_ref, vt_ref, km_ref, ysg_ref, ycv_ref, ext_ref) = refs
    else:
        (x_ref, w_ref, wvt_ref, cos_ref, sin_ref, sglng_ref, sglnb_ref, sgw_ref, sgb_ref,
         cw_ref, cb_ref, clng_ref, clnb_ref,
         q_ref, kx_ref, vt_ref, km_ref, ysg_ref, ycv_ref, ext_ref) = refs
    i = pl.program_id(0)
    h = x_ref[...]
    if apply_ln:
        h = _layer_norm(h, lng_ref[...], lnb_ref[...])
        h_ref[...] = h
    hb = h.astype(BF16)
    p = _dot(hb, w_ref[...])
    vt_ref[:, 0:HEAD_DIM, :] = _dot_nt(wvt_ref[...], hb).reshape(ATT_HEADS, HEAD_DIM, ROW_TILE).astype(BF16)
    vt_ref[:, HEAD_DIM:, :] = jnp.ones((ATT_HEADS, VT_ROWS - HEAD_DIM, ROW_TILE), BF16)

    lane = lax.broadcasted_iota(jnp.int32, (ROW_TILE, LANES), 1)
    first_half = (lane % HEAD_DIM) < (HEAD_DIM // 2)
    low_head = lane < HEAD_DIM
    cos = cos_ref[...]
    sin = sin_ref[...]

    def rot(x):
        swapped = jnp.where(first_half, pltpu.roll(x, LANES - HEAD_DIM // 2, 1), pltpu.roll(x, HEAD_DIM // 2, 1))
        return x * cos + swapped * sin

    scale = HEAD_DIM ** -0.5 * LOG2_E
    for j in range(ATT_WIDTH // LANES):
        sl = slice(j * LANES, (j + 1) * LANES)
        q_ref[:, sl] = (rot(p[:, sl]) * scale).astype(BF16)
        kr = rot(p[:, K_OFF + j * LANES:K_OFF + (j + 1) * LANES])
        km_ref[0, :, sl] = jnp.broadcast_to(jnp.mean(kr, axis=0, keepdims=True), (SUBLANES, LANES))
        kx_ref[2 * j] = jnp.where(low_head, kr, (lane - HEAD_DIM == i).astype(F32)).astype(BF16)
        kx_ref[2 * j + 1] = jnp.where(low_head, (lane == i).astype(F32), kr).astype(BF16)

    u = jax.nn.gelu(p[:, P_SGU:P_SGV])
    vv = _layer_norm(jax.nn.gelu(p[:, P_SGV:P_CA]), sglng_ref[...], sglnb_ref[...]).astype(BF16)
    r_i = lax.broadcasted_iota(jnp.int32, (SG_CHUNK, SG_CHUNK), 0)
    c_i = lax.broadcasted_iota(jnp.int32, (SG_CHUNK, SG_CHUNK), 1)
    lane_sg = lax.broadcasted_iota(jnp.int32, (SG_CHUNK, SG_WIDTH), 1)
    for c in range(ROW_TILE // SG_CHUNK):
        rows = slice(c * SG_CHUNK, (c + 1) * SG_CHUNK)
        mixed = sgb_ref[...]
        for g in range(SG_GROUPS):
            wg = jnp.where(c_i <= r_i, sgw_ref[g], 0.0).astype(BF16)
            full = _dot(wg, vv[rows])
            mixed = mixed + jnp.where(lane_sg // SG_GROUP_DIM == g, full, 0.0)
        ysg_ref[rows, :] = (u[rows] * mixed).astype(BF16)

    @pl.when(i == 0)
    def _():
        ext_ref[0:CONV_HALO, :] = jnp.zeros((CONV_HALO, CONV_WIDTH), F32)

    ext_ref[CONV_HALO:, :] = p[:, P_CA:P_CG] * jax.nn.sigmoid(p[:, P_CG:P_COLS])
    acc = jnp.zeros((ROW_TILE, CONV_WIDTH), F32) + cb_ref[...]
    first = CONV_HALO - (CONV_TAPS - 1)
    ext = ext_ref[...]
    n_ext = CONV_HALO + ROW_TILE
    for shift in range(SUBLANES):
        shifted = ext if shift == 0 else pltpu.roll(ext, n_ext - shift, 0)
        for o in range(first + (shift - first) % SUBLANES, first + CONV_TAPS, SUBLANES):
            acc = acc + cw_ref[o - first:o - first + 1, :] * shifted[o - shift:o - shift + ROW_TILE]
    ext_ref[0:CONV_HALO, :] = ext_ref[ROW_TILE:ROW_TILE + CONV_HALO, :]
    hc = _layer_norm(acc, clng_ref[...], clnb_ref[...])
    ycv_ref[...] = (hc * jax.nn.sigmoid(hc)).astype(BF16)


def _front(apply_ln, x, ln_g, ln_b, w_p, w_vt, cos, sin, sg_ln_g, sg_ln_b, sg_w, sg_bias, conv_w, conv_b,
           conv_ln_g, conv_ln_b):
    n_tiles = SEQ // ROW_TILE
    row = lambda w: pl.BlockSpec((ROW_TILE, w), lambda i: (i, 0))
    full = lambda shape: pl.BlockSpec(shape, lambda i: (0,) * len(shape))
    in_specs = [row(D_MODEL)]
    args = [x]
    if apply_ln:
        in_specs += [full((1, D_MODEL)), full((1, D_MODEL))]
        args += [ln_g, ln_b]
    in_specs += [full((D_MODEL, P_COLS)), full((ATT_WIDTH, D_MODEL)), row(LANES), row(LANES),
                 full((1, SG_WIDTH)), full((1, SG_WIDTH)),
                 full((SG_GROUPS, SG_CHUNK, SG_CHUNK)), full((SG_CHUNK, SG_WIDTH)),
                 full((CONV_TAPS, CONV_WIDTH)), full((1, CONV_WIDTH)), full((1, CONV_WIDTH)),
                 full((1, CONV_WIDTH))]
    args += [w_p, w_vt, cos, sin, sg_ln_g, sg_ln_b, sg_w, sg_bias, conv_w, conv_b, conv_ln_g, conv_ln_b]
    out_shape = [jax.ShapeDtypeStruct((SEQ, ATT_WIDTH), BF16),
                 jax.ShapeDtypeStruct((ATT_HEADS, SEQ, LANES), BF16),
                 jax.ShapeDtypeStruct((ATT_HEADS, VT_ROWS, SEQ), BF16),
                 jax.ShapeDtypeStruct((N_BLOCKS, SUBLANES, ATT_WIDTH), F32),
                 jax.ShapeDtypeStruct((SEQ, SG_WIDTH), BF16),
                 jax.ShapeDtypeStruct((SEQ, CONV_WIDTH), BF16)]
    out_specs = [row(ATT_WIDTH), pl.BlockSpec((ATT_HEADS, ROW_TILE, LANES), lambda i: (0, i, 0)),
                 pl.BlockSpec((ATT_HEADS, VT_ROWS, ROW_TILE), lambda i: (0, 0, i)),
                 pl.BlockSpec((1, SUBLANES, ATT_WIDTH), lambda i: (i, 0, 0)),
                 row(SG_WIDTH), row(CONV_WIDTH)]
    if apply_ln:
        out_shape = [jax.ShapeDtypeStruct((SEQ, D_MODEL), F32)] + out_shape
        out_specs = [row(D_MODEL)] + out_specs
    return pl.pallas_call(
        functools.partial(_front_kernel, apply_ln),
        grid=(n_tiles,),
        in_specs=in_specs,
        out_specs=out_specs,
        out_shape=out_shape,
        scratch_shapes=[pltpu.VMEM((CONV_HALO + ROW_TILE, CONV_WIDTH), F32)],
        compiler_params=pltpu.CompilerParams(dimension_semantics=("arbitrary",), vmem_limit_bytes=VMEM_LIMIT),
        name="front_ln" if apply_ln else "front",
    )(*args)


def _attn_kernel(q_ref, kx_ref, vt_ref, km_ref, o_ref, s0_ref, s1_ref, mx0_ref, mx1_ref):
    first_blk = pl.program_id(1) * ATT_Q_BLOCKS
    q = q_ref[...]

    km = km_ref[...]
    km_lane = lax.broadcasted_iota(jnp.int32, (N_BLOCKS, LANES), 1)
    km_rows = jnp.concatenate([jnp.where(km_lane >= HEAD_DIM, km, 0.0), jnp.where(km_lane < HEAD_DIM, km, 0.0)],
                              axis=0)
    km_hi, km_lo = _split_bf16(km_rows)
    eye = (lax.broadcasted_iota(jnp.int32, (LANES, LANES), 0)
           == lax.broadcasted_iota(jnp.int32, (LANES, LANES), 1)).astype(F32).astype(BF16)
    res = _dot_nt(jnp.concatenate([km_hi, km_lo, eye], axis=0), q)
    q_t = res[2 * LANES:].astype(BF16)
    gate = (res[0:LANES] + res[LANES:2 * LANES]).reshape(2, N_BLOCKS, ATT_Q_TILE)
    blk = lax.broadcasted_iota(jnp.int32, (2, N_BLOCKS, ATT_Q_TILE), 1)
    own = first_blk + lax.broadcasted_iota(jnp.int32, (2, N_BLOCKS, ATT_Q_TILE), 2) // MOBA_BLOCK
    gate = jnp.where(blk < own, gate, NEG)

    sel = jnp.zeros((2, N_BLOCKS, ATT_Q_TILE), jnp.bool_)
    blk_f = blk.astype(F32)
    for r in range(MOBA_TOPK):
        best = jnp.max(gate, axis=1, keepdims=True)
        first = jnp.min(jnp.where(gate == best, blk_f, float(N_BLOCKS)), axis=1, keepdims=True)
        pick = blk_f == first
        sel = sel | (pick & (own > r))
        gate = jnp.where(pick, -jnp.inf, gate)
    bias_t = jnp.where(sel, 0.0, NEG).astype(BF16).reshape(LANES, ATT_Q_TILE)
    low = lax.broadcasted_iota(jnp.int32, (LANES, ATT_Q_TILE), 0) < HEAD_DIM
    zero = jnp.zeros_like(q_t)
    qx_past = (jnp.where(low, q_t, bias_t), jnp.where(low, bias_t, q_t))
    qx_own = (jnp.where(low, q_t, zero), jnp.where(low, zero, q_t))

    key_i = lax.broadcasted_iota(jnp.int32, (ATT_Q_TILE, ATT_Q_TILE), 0)
    qry_i = lax.broadcasted_iota(jnp.int32, (ATT_Q_TILE, ATT_Q_TILE), 1)
    own_mask = (key_i <= qry_i) & (key_i // MOBA_BLOCK == qry_i // MOBA_BLOCK)
    start_own = pl.multiple_of(first_blk * MOBA_BLOCK, ATT_Q_TILE)

    state = []
    for par in range(2):
        s = _dot(kx_ref[par, pl.ds(start_own, ATT_Q_TILE), :], qx_own[par])
        s = jnp.where(own_mask, s, NEG)
        m = jnp.max(s, axis=0, keepdims=True)
        p = jnp.exp2(s - m)
        acc = _dot(vt_ref[par, :, pl.ds(start_own, ATT_Q_TILE)], p.astype(BF16))
        state += [m, acc]

    n_groups = lax.shift_right_logical(first_blk + (ATT_Q_BLOCKS - 1) + (ATT_GROUP - 1), ATT_GROUP_SHIFT)

    def scores(g, s_ref, mx_ref):
        start = pl.multiple_of(g * ATT_GROUP_KEYS, ATT_GROUP_KEYS)
        for par in range(2):
            s = _dot(kx_ref[par, pl.ds(start, ATT_GROUP_KEYS), :], qx_past[par])
            s_ref[par] = s
            mx_ref[par] = jnp.max(s, axis=0, keepdims=True)

    def consume(g, s_ref, mx_ref, carry):
        start = pl.multiple_of(g * ATT_GROUP_KEYS, ATT_GROUP_KEYS)
        out = []
        for par in range(2):
            m, acc = carry[2 * par:2 * par + 2]
            m_new = jnp.maximum(m, mx_ref[par])
            a = jnp.exp2(m - m_new)
            p = jnp.exp2(s_ref[par] - m_new)
            acc = a * acc + _dot(vt_ref[par, :, pl.ds(start, ATT_GROUP_KEYS)], p.astype(BF16))
            out += [m_new, acc]
        return tuple(out)

    scores(0, s0_ref, mx0_ref)

    def body(g, carry):
        def even(c):
            scores(g + 1, s1_ref, mx1_ref)
            return consume(g, s0_ref, mx0_ref, c)

        def odd(c):
            scores(g + 1, s0_ref, mx0_ref)
            return consume(g, s1_ref, mx1_ref, c)

        return lax.cond((g & 1) == 0, even, odd, carry)

    last = n_groups - 1
    state = lax.fori_loop(0, last, body, tuple(state))
    state = lax.cond((last & 1) == 0, lambda c: consume(last, s0_ref, mx0_ref, c),
                     lambda c: consume(last, s1_ref, mx1_ref, c), state)
    o_t = jnp.concatenate([acc[0:HEAD_DIM] / acc[HEAD_DIM:HEAD_DIM + 1] for acc in (state[1], state[3])],
                          axis=0)
    o_ref[...] = o_t.T.astype(BF16)


def _attn(q, kx, vt, kmean):
    n_pairs = ATT_WIDTH // LANES
    return pl.pallas_call(
        _attn_kernel,
        grid=(n_pairs, SEQ // ATT_Q_TILE),
        in_specs=[pl.BlockSpec((ATT_Q_TILE, LANES), lambda j, t: (t, j)),
                  pl.BlockSpec((2, SEQ, LANES), lambda j, t: (j, 0, 0)),
                  pl.BlockSpec((2, VT_ROWS, SEQ), lambda j, t: (j, 0, 0)),
                  pl.BlockSpec((N_BLOCKS, LANES), lambda j, t: (0, j))],
        out_specs=pl.BlockSpec((ATT_Q_TILE, LANES), lambda j, t: (t, j)),
        out_shape=jax.ShapeDtypeStruct((SEQ, ATT_WIDTH), BF16),
        scratch_shapes=([pltpu.VMEM((2, ATT_GROUP_KEYS, ATT_Q_TILE), F32)] * 2
                        + [pltpu.VMEM((2, 1, ATT_Q_TILE), F32)] * 2),
        compiler_params=pltpu.CompilerParams(dimension_semantics=("arbitrary", "arbitrary"),
                                             vmem_limit_bytes=VMEM_LIMIT),
        name="attn",
    )(q, kx, vt, kmean)


def _outproj_kernel(h_ref, ya_ref, ys_ref, yc_ref, wo_ref, g_ref, b_ref, wr_ref, rb_ref,
                    h1x_ref, grp_ref, rank_ref, cnt_ref):
    mix = (_dot(ya_ref[...], wo_ref[0:ATT_WIDTH, :])
           + _dot(ys_ref[...], wo_ref[ATT_WIDTH:ATT_WIDTH + SG_WIDTH, :])
           + _dot(yc_ref[...], wo_ref[ATT_WIDTH + SG_WIDTH:, :]))
    h1 = _layer_norm(DEEPNORM_ALPHA * h_ref[...] + mix, g_ref[...], b_ref[...])
    h1x_ref[:, 0:D_MODEL] = h1

    w_hi, w_lo = _split_bf16(wr_ref[...])
    h_hi, h_lo = _split_bf16(h1)
    both = _dot_nt(jnp.concatenate([w_hi, w_lo], axis=0), h_hi)
    logits = both[0:N_EXPERTS] + both[N_EXPERTS:] + _dot_nt(w_hi, h_lo)
    scores = jax.nn.sigmoid(logits)
    biased = scores + rb_ref[...]
    sc = [scores[e:e + 1, :] for e in range(N_EXPERTS)]
    bi = [biased[e:e + 1, :] for e in range(N_EXPERTS)]

    best = None
    for g in range(N_EXPERT_GROUPS):
        a0, a1, a2, a3 = bi[4 * g:4 * g + 4]
        hi01, lo01 = jnp.maximum(a0, a1), jnp.minimum(a0, a1)
        hi23, lo23 = jnp.maximum(a2, a3), jnp.minimum(a2, a3)
        top1 = jnp.maximum(hi01, hi23)
        top2 = jnp.maximum(jnp.minimum(hi01, hi23), jnp.maximum(lo01, lo23))
        gs = top1 + top2
        if best is None:
            best, grp = gs, jnp.zeros_like(gs, dtype=jnp.int32)
        else:
            upd = gs > best
            best = jnp.where(upd, gs, best)
            grp = jnp.where(upd, g, grp)

    def in_group(vals, k):
        out = vals[k]
        for g in range(1, N_EXPERT_GROUPS):
            out = jnp.where(grp == g, vals[4 * g + k], out)
        return out

    vb = [in_group(bi, k) for k in range(EXPERTS_PER_GROUP)]
    vs = [in_group(sc, k) for k in range(EXPERTS_PER_GROUP)]
    m1, i1, s1 = vb[0], jnp.zeros_like(grp), vs[0]
    for k in range(1, EXPERTS_PER_GROUP):
        upd = vb[k] > m1
        m1 = jnp.where(upd, vb[k], m1)
        i1 = jnp.where(upd, k, i1)
        s1 = jnp.where(upd, vs[k], s1)
    m2 = jnp.full_like(m1, -jnp.inf)
    i2 = jnp.zeros_like(grp)
    s2 = jnp.zeros_like(s1)
    for k in range(EXPERTS_PER_GROUP):
        upd = (i1 != k) & (vb[k] > m2)
        m2 = jnp.where(upd, vb[k], m2)
        i2 = jnp.where(upd, k, i2)
        s2 = jnp.where(upd, vs[k], s2)
    tot = s1 + s2
    row = lax.broadcasted_iota(jnp.int32, (GATE_ROWS, ROW_TILE), 0)
    rest = (jnp.where(row % EXPERTS_PER_GROUP == i1, s1 / tot, 0.0)
            + jnp.where(row % EXPERTS_PER_GROUP == i2, s2 / tot, 0.0))
    pieces = jnp.zeros((GATE_ROWS, ROW_TILE), F32)
    for i in range(GATE_PIECES):
        piece = rest.astype(BF16).astype(F32)
        pieces = jnp.where(row // EXPERTS_PER_GROUP == i, piece, pieces)
        rest = rest - piece
    t_r = lax.broadcasted_iota(jnp.int32, (ROW_TILE, ROW_TILE), 0)
    t_c = lax.broadcasted_iota(jnp.int32, (ROW_TILE, ROW_TILE), 1)
    pieces_t = _dot_nt(jnp.where(t_r == t_c, 1.0, 0.0).astype(BF16), pieces.astype(BF16))
    p_r = lax.broadcasted_iota(jnp.int32, (GATE_ROWS, LANES), 0)
    p_c = lax.broadcasted_iota(jnp.int32, (GATE_ROWS, LANES), 1)
    add_pieces = jnp.where((p_r % EXPERTS_PER_GROUP == p_c) & (p_r < GATE_PIECES * EXPERTS_PER_GROUP), 1.0, 0.0)
    h1x_ref[:, D_MODEL:] = _dot(pieces_t.astype(BF16), add_pieces.astype(BF16))
    grp_ref[...] = grp
    k_idx = lax.broadcasted_iota(jnp.int32, (EXPERTS_PER_GROUP, ROW_TILE), 0)

    @pl.when(pl.program_id(0) == 0)
    def _():
        cnt_ref[...] = jnp.zeros_like(cnt_ref)

    member = (k_idx == grp).astype(F32)
    earlier = _dot(member.astype(BF16), (t_r < t_c).astype(BF16)) + cnt_ref[...]
    rank_ref[...] = jnp.sum(member * earlier, axis=0, keepdims=True).astype(jnp.int32)
    cnt_ref[...] += jnp.sum(member, axis=1, keepdims=True)


def _outproj(h, y_att, y_sg, y_cv, w_out, ln_g, ln_b, w_router_t, router_bias):
    row = lambda w: pl.BlockSpec((ROW_TILE, w), lambda i: (i, 0))
    full = lambda shape: pl.BlockSpec(shape, lambda i: (0,) * len(shape))
    lane_row = lambda n: pl.BlockSpec((n, ROW_TILE), lambda i: (0, i))
    return pl.pallas_call(
        _outproj_kernel,
        grid=(SEQ // ROW_TILE,),
        in_specs=[row(D_MODEL), row(ATT_WIDTH), row(SG_WIDTH), row(CONV_WIDTH), full((D_MODEL, D_MODEL)),
                  full((1, D_MODEL)), full((1, D_MODEL)), full((N_EXPERTS, D_MODEL)), full((N_EXPERTS, 1))],
        out_specs=[row(MOE_ROW_WIDTH), lane_row(1), lane_row(1)],
        out_shape=[jax.ShapeDtypeStruct((SEQ, MOE_ROW_WIDTH), F32),
                   jax.ShapeDtypeStruct((1, SEQ), jnp.int32),
                   jax.ShapeDtypeStruct((1, SEQ), jnp.int32)],
        scratch_shapes=[pltpu.VMEM((N_EXPERT_GROUPS, ROW_TILE), F32)],
        compiler_params=pltpu.CompilerParams(dimension_semantics=("arbitrary",), vmem_limit_bytes=VMEM_LIMIT),
        name="outproj",
    )(h, y_att, y_sg, y_cv, w_out, ln_g, ln_b, w_router_t, router_bias)


def _moe_plan(grp, rank):
    member = (grp[:, None] == jnp.arange(N_EXPERT_GROUPS)[None, :]).astype(jnp.int32)
    tiles = (member.sum(axis=0) + MOE_TILE - 1) // MOE_TILE
    tile0 = jnp.cumsum(tiles) - tiles
    pos = tile0[grp] * MOE_TILE + rank
    src = jnp.zeros((MOE_SORTED_ROWS,), jnp.int32).at[pos].set(
        jnp.arange(SEQ, dtype=jnp.int32), unique_indices=True, mode="promise_in_bounds")
    n_tiles = tiles.sum()
    tile_idx = jnp.minimum(jnp.arange(MOE_MAX_TILES), n_tiles - 1)
    tile_grp = (tile_idx[:, None] >= tile0[None, 1:]).sum(axis=1)
    return pos, src, tile_grp.astype(jnp.int32), n_tiles.reshape(1).astype(jnp.int32)


def _row_copy(src_hbm, buf, sem, slot, src_row, dst_row, n_rows):
    return pltpu.make_async_copy(src_hbm.at[pl.ds(src_row, n_rows), :], buf.at[slot, pl.ds(dst_row, n_rows), :],
                                 sem.at[slot])


def _start_row_gather(idx_ref, src_hbm, buf, sem, tile, slot, n_rows):
    def body(r, carry):
        _row_copy(src_hbm, buf, sem, slot, idx_ref[tile * n_rows + r], r, 1).start()
        return carry
    lax.fori_loop(0, n_rows, body, 0, unroll=8)


def _moe_ffn_kernel(src_ref, grp_ref, nt_ref, hx_hbm, wg_ref, wu_ref, wd_ref, y_ref, buf, sem):
    i = pl.program_id(0)
    n_tiles = nt_ref[0]
    slot = i & 1

    @pl.when(i == 0)
    def _():
        _start_row_gather(src_ref, hx_hbm, buf, sem, 0, 0, MOE_TILE)

    @pl.when(i + 1 < n_tiles)
    def _():
        _start_row_gather(src_ref, hx_hbm, buf, sem, i + 1, 1 - slot, MOE_TILE)

    @pl.when(i < n_tiles)
    def _():
        _row_copy(hx_hbm, buf, sem, slot, 0, 0, MOE_TILE).wait()
        x = buf[slot, :, 0:D_MODEL].astype(BF16)
        w = buf[slot, :, D_MODEL:]
        y = jnp.zeros((MOE_TILE, D_MODEL), F32)
        for k in range(EXPERTS_PER_GROUP):
            a = jax.nn.silu(_dot(x, wg_ref[0, k])) * _dot(x, wu_ref[0, k])
            y = y + w[:, k:k + 1] * _dot(a.astype(BF16), wd_ref[0, k])
        y_ref[...] = y

    @pl.when(i >= n_tiles)
    def _():
        y_ref[...] = jnp.zeros_like(y_ref)


def _moe_ffn(layer, src, tile_grp, n_tiles, hx, w_gate, w_up, w_down):
    experts = lambda a, b: pl.BlockSpec((1, EXPERTS_PER_GROUP, a, b),
                                        lambda i, src, grp, nt: (layer, grp[i], 0, 0))
    return pl.pallas_call(
        _moe_ffn_kernel,
        grid_spec=pltpu.PrefetchScalarGridSpec(
            num_scalar_prefetch=3,
            grid=(MOE_MAX_TILES,),
            in_specs=[pl.BlockSpec(memory_space=pl.ANY),
                      experts(D_MODEL, D_EXPERT), experts(D_MODEL, D_EXPERT), experts(D_EXPERT, D_MODEL)],
            out_specs=pl.BlockSpec((MOE_TILE, D_MODEL), lambda i, src, grp, nt: (i, 0)),
            scratch_shapes=[pltpu.VMEM((2, MOE_TILE, MOE_ROW_WIDTH), F32), pltpu.SemaphoreType.DMA((2,))]),
        out_shape=jax.ShapeDtypeStruct((MOE_SORTED_ROWS, D_MODEL), F32),
        compiler_params=pltpu.CompilerParams(dimension_semantics=("arbitrary",), vmem_limit_bytes=VMEM_LIMIT),
        name="moe_ffn",
    )(src, tile_grp, n_tiles, hx, w_gate, w_up, w_down)


def _moe_combine_kernel(pos_ref, y_hbm, h_ref, g_ref, b_ref, o_ref, buf, sem):
    i = pl.program_id(0)
    slot = i & 1

    @pl.when(i == 0)
    def _():
        _start_row_gather(pos_ref, y_hbm, buf, sem, 0, 0, MOE_SRC_TILE)

    @pl.when(i + 1 < pl.num_programs(0))
    def _():
        _start_row_gather(pos_ref, y_hbm, buf, sem, i + 1, 1 - slot, MOE_SRC_TILE)

    _row_copy(y_hbm, buf, sem, slot, 0, 0, MOE_SRC_TILE).wait()
    o_ref[...] = _layer_norm(DEEPNORM_ALPHA * h_ref[...] + buf[slot], g_ref[...], b_ref[...])


def _moe_combine(y_sorted, pos, h, ln_g, ln_b):
    src_rows = lambda w: pl.BlockSpec((MOE_SRC_TILE, w), lambda i, pos: (i, 0))
    const = lambda shape: pl.BlockSpec(shape, lambda i, pos: (0,) * len(shape))
    return pl.pallas_call(
        _moe_combine_kernel,
        grid_spec=pltpu.PrefetchScalarGridSpec(
            num_scalar_prefetch=1,
            grid=(SEQ // MOE_SRC_TILE,),
            in_specs=[pl.BlockSpec(memory_space=pl.ANY), src_rows(D_MODEL), const((1, D_MODEL)),
                      const((1, D_MODEL))],
            out_specs=src_rows(D_MODEL),
            scratch_shapes=[pltpu.VMEM((2, MOE_SRC_TILE, D_MODEL), F32), pltpu.SemaphoreType.DMA((2,))]),
        out_shape=jax.ShapeDtypeStruct((SEQ, D_MODEL), F32),
        compiler_params=pltpu.CompilerParams(dimension_semantics=("arbitrary",), vmem_limit_bytes=VMEM_LIMIT),
        name="moe_combine",
    )(pos, y_sorted, h, ln_g, ln_b)


def _rope_tables():
    half = HEAD_DIM // 2
    inv_freq = ROPE_THETA ** (-jnp.arange(half, dtype=F32) / half)
    ang = jnp.arange(SEQ, dtype=F32)[:, None] * inv_freq[None, :]
    cos, sin = jnp.cos(ang), jnp.sin(ang)
    reps = LANES // HEAD_DIM
    return jnp.tile(jnp.concatenate([cos, cos], axis=1), (1, reps)), jnp.tile(jnp.concatenate([-sin, sin], axis=1), (1, reps))


def kernel(x, ln_in_g, ln_in_b, w_in, w_out, sg_ln_g, sg_ln_b, sg_w, sg_b, conv_w, conv_b, conv_ln_g, conv_ln_b, ln_mix_g, ln_mix_b, w_router, router_bias, w_gate, w_up, w_down, ln_ffn_g, ln_ffn_b):
    assert x.shape == (1, SEQ, D_MODEL) and w_in.shape == (DEPTH, D_MODEL, IN_COLS)
    cos, sin = _rope_tables()
    row = lambda a: a.reshape(1, -1)
    w_router_t = w_router.T
    rb = router_bias.reshape(N_EXPERTS, 1)
    w_gate_bf16, w_up_bf16, w_down_bf16 = w_gate.astype(BF16), w_up.astype(BF16), w_down.astype(BF16)
    h = x.reshape(SEQ, D_MODEL)
    for l in range(DEPTH):
        sg_bias = jnp.repeat(sg_b[l].T, SG_GROUP_DIM, axis=1)
        w_p = jnp.concatenate([w_in[l][:, :V_OFF], w_in[l][:, SGU_OFF:]], axis=1).astype(BF16)
        w_vt = w_in[l][:, V_OFF:SGU_OFF].T.astype(BF16)
        args = (w_p, w_vt, cos, sin, row(sg_ln_g[l]), row(sg_ln_b[l]), sg_w[l], sg_bias,
                conv_w[l], row(conv_b[l]), row(conv_ln_g[l]), row(conv_ln_b[l]))
        if l == 0:
            h, q, kx, vt, kmean, y_sg, y_cv = _front(True, h, row(ln_in_g), row(ln_in_b), *args)
        else:
            q, kx, vt, kmean, y_sg, y_cv = _front(False, h, None, None, *args)
        y_att = _attn(q, kx, vt, kmean[:, 0, :])
        hx, grp, rank = _outproj(h, y_att, y_sg, y_cv, w_out[l].astype(BF16), row(ln_mix_g[l]),
                                 row(ln_mix_b[l]), w_router_t, rb)
        pos, src, tile_grp, n_tiles = _moe_plan(grp[0], rank[0])
        y_sorted = _moe_ffn(l, src, tile_grp, n_tiles, hx, w_gate_bf16, w_up_bf16, w_down_bf16)
        h = _moe_combine(y_sorted, pos, hx, row(ln_ffn_g[l]), row(ln_ffn_b[l]))
    return h.reshape(1, SEQ, D_MODEL)
```

```python
import functools
import math

import jax
import jax.numpy as jnp
from jax import lax
from jax.experimental import pallas as pl
from jax.experimental.pallas import tpu as pltpu

D_MODEL = 1024
SEQ = 16384
DEPTH = 2
HEAD_DIM = 64
ATT_WIDTH = 512
ATT_HEADS = 8
SG_WIDTH = 256
SG_GROUPS = 4
SG_GROUP_DIM = 64
CONV_WIDTH = 256
K_OFF = 512
V_OFF = 1024
SGU_OFF = 1536
SGV_OFF = 1792
CA_OFF = 2048
CG_OFF = 2304
IN_COLS = 2560
P_SGU = SGU_OFF - ATT_WIDTH
P_SGV = SGV_OFF - ATT_WIDTH
P_CA = CA_OFF - ATT_WIDTH
P_CG = CG_OFF - ATT_WIDTH
P_COLS = IN_COLS - ATT_WIDTH
MOBA_BLOCK = 256
MOBA_TOPK = 3
N_BLOCKS = SEQ // MOBA_BLOCK
SG_CHUNK = 128
CONV_TAPS = 31
ROPE_THETA = 10000.0
N_EXPERTS = 16
N_EXPERT_GROUPS = 4
EXPERTS_PER_GROUP = 4
D_EXPERT = 512
LN_EPS = 1e-5
NEG = -1e30
DEEPNORM_ALPHA = (2.0 * DEPTH) ** 0.25

LANES = 128
SUBLANES = 8
ROW_TILE = MOBA_BLOCK
CONV_HALO = 32
LOG2_E = math.log2(math.e)
VT_ROWS = HEAD_DIM + 16
ATT_GROUP_SHIFT = 2
ATT_GROUP = 1 << ATT_GROUP_SHIFT
ATT_Q_BLOCKS = 2
ATT_Q_TILE = ATT_Q_BLOCKS * MOBA_BLOCK
ATT_GROUP_KEYS = ATT_GROUP * MOBA_BLOCK
MOE_ROW_WIDTH = D_MODEL + LANES
GATE_PIECES = 3
GATE_ROWS = 16
MOE_TILE = 256
MOE_SRC_TILE = 512
MOE_MAX_TILES = SEQ // MOE_TILE + N_EXPERT_GROUPS
MOE_SORTED_ROWS = MOE_MAX_TILES * MOE_TILE
VMEM_LIMIT = 48 * 1024 * 1024

F32 = jnp.float32
BF16 = jnp.bfloat16


def _dot(a, b):
    return jnp.dot(a, b, preferred_element_type=F32)


def _dot_nt(a, b):
    return lax.dot_general(a, b, (((1,), (1,)), ((), ())), preferred_element_type=F32)


def _layer_norm(x, g, b):
    mu = jnp.mean(x, axis=-1, keepdims=True)
    xc = x - mu
    var = jnp.mean(xc * xc, axis=-1, keepdims=True)
    return xc * lax.rsqrt(var + LN_EPS) * g + b


def _split_bf16(x):
    hi = x.astype(BF16)
    lo = (x - hi.astype(F32)).astype(BF16)
    return hi, lo


def _front_kernel(apply_ln, *refs):
    if apply_ln:
        (x_ref, lng_ref, lnb_ref, w_ref, wvt_ref, cos_ref, sin_ref, sglng_ref, sglnb_ref, sgw_ref, sgb_ref,
         cw_ref, cb_ref, clng_ref, clnb_ref,
         h_ref, q_ref, kx_ref, vt_ref, km_ref, ysg_ref, ycv_ref, ext_ref) = refs
    else:
        (x_ref, w_ref, wvt_ref, cos_ref, sin_ref, sglng_ref, sglnb_ref, sgw_ref, sgb_ref,
         cw_ref, cb_ref, clng_ref, clnb_ref,
         q_ref, kx_ref, vt_ref, km_ref, ysg_ref, ycv_ref, ext_ref) = refs
    i = pl.program_id(0)
    h = x_ref[...]
    if apply_ln:
        h = _layer_norm(h, lng_ref[...], lnb_ref[...])
        h_ref[...] = h
    hb = h.astype(BF16)
    p = _dot(hb, w_ref[...])
    vt_ref[:, 0:HEAD_DIM, :] = _dot_nt(wvt_ref[...], hb).reshape(ATT_HEADS, HEAD_DIM, ROW_TILE).astype(BF16)
    vt_ref[:, HEAD_DIM:, :] = jnp.ones((ATT_HEADS, VT_ROWS - HEAD_DIM, ROW_TILE), BF16)

    lane = lax.broadcasted_iota(jnp.int32, (ROW_TILE, LANES), 1)
    first_half = (lane % HEAD_DIM) < (HEAD_DIM // 2)
    low_head = lane < HEAD_DIM
    cos = cos_ref[...]
    sin = sin_ref[...]

    def rot(x):
        swapped = jnp.where(first_half, pltpu.roll(x, LANES - HEAD_DIM // 2, 1), pltpu.roll(x, HEAD_DIM // 2, 1))
        return x * cos + swapped * sin

    scale = HEAD_DIM ** -0.5 * LOG2_E
    for j in range(ATT_WIDTH // LANES):
        sl = slice(j * LANES, (j + 1) * LANES)
        q_ref[:, sl] = (rot(p[:, sl]) * scale).astype(BF16)
        kr = rot(p[:, K_OFF + j * LANES:K_OFF + (j + 1) * LANES])
        km_ref[0, :, sl] = jnp.broadcast_to(jnp.mean(kr, axis=0, keepdims=True), (SUBLANES, LANES))
        kx_ref[2 * j] = jnp.where(low_head, kr, (lane - HEAD_DIM == i).astype(F32)).astype(BF16)
        kx_ref[2 * j + 1] = jnp.where(low_head, (lane == i).astype(F32), kr).astype(BF16)

    u = jax.nn.gelu(p[:, P_SGU:P_SGV])
    vv = _layer_norm(jax.nn.gelu(p[:, P_SGV:P_CA]), sglng_ref[...], sglnb_ref[...]).astype(BF16)
    r_i = lax.broadcasted_iota(jnp.int32, (SG_CHUNK, SG_CHUNK), 0)
    c_i = lax.broadcasted_iota(jnp.int32, (SG_CHUNK, SG_CHUNK), 1)
    lane_sg = lax.broadcasted_iota(jnp.int32, (SG_CHUNK, SG_WIDTH), 1)
    for c in range(ROW_TILE // SG_CHUNK):
        rows = slice(c * SG_CHUNK, (c + 1) * SG_CHUNK)
        mixed = sgb_ref[...]
        for g in range(SG_GROUPS):
            wg = jnp.where(c_i <= r_i, sgw_ref[g], 0.0).astype(BF16)
            full = _dot(wg, vv[rows])
            mixed = mixed + jnp.where(lane_sg // SG_GROUP_DIM == g, full, 0.0)
        ysg_ref[rows, :] = (u[rows] * mixed).astype(BF16)

    @pl.when(i == 0)
    def _():
        ext_ref[0:CONV_HALO, :] = jnp.zeros((CONV_HALO, CONV_WIDTH), F32)

    ext_ref[CONV_HALO:, :] = p[:, P_CA:P_CG] * jax.nn.sigmoid(p[:, P_CG:P_COLS])
    acc = jnp.zeros((ROW_TILE, CONV_WIDTH), F32) + cb_ref[...]
    first = CONV_HALO - (CONV_TAPS - 1)
    ext = ext_ref[...]
    n_ext = CONV_HALO + ROW_TILE
    for shift in range(SUBLANES):
        shifted = ext if shift == 0 else pltpu.roll(ext, n_ext - shift, 0)
        for o in range(first + (shift - first) % SUBLANES, first + CONV_TAPS, SUBLANES):
            acc = acc + cw_ref[o - first:o - first + 1, :] * shifted[o - shift:o - shift + ROW_TILE]
    ext_ref[0:CONV_HALO, :] = ext_ref[ROW_TILE:ROW_TILE + CONV_HALO, :]
    hc = _layer_norm(acc, clng_ref[...], clnb_ref[...])
    ycv_ref[...] = (hc * jax.nn.sigmoid(hc)).astype(BF16)


def _front(apply_ln, x, ln_g, ln_b, w_p, w_vt, cos, sin, sg_ln_g, sg_ln_b, sg_w, sg_bias, conv_w, conv_b,
           conv_ln_g, conv_ln_b):
    n_tiles = SEQ // ROW_TILE
    row = lambda w: pl.BlockSpec((ROW_TILE, w), lambda i: (i, 0))
    full = lambda shape: pl.BlockSpec(shape, lambda i: (0,) * len(shape))
    in_specs = [row(D_MODEL)]
    args = [x]
    if apply_ln:
        in_specs += [full((1, D_MODEL)), full((1, D_MODEL))]
        args += [ln_g, ln_b]
    in_specs += [full((D_MODEL, P_COLS)), full((ATT_WIDTH, D_MODEL)), row(LANES), row(LANES),
                 full((1, SG_WIDTH)), full((1, SG_WIDTH)),
                 full((SG_GROUPS, SG_CHUNK, SG_CHUNK)), full((SG_CHUNK, SG_WIDTH)),
                 full((CONV_TAPS, CONV_WIDTH)), full((1, CONV_WIDTH)), full((1, CONV_WIDTH)),
                 full((1, CONV_WIDTH))]
    args += [w_p, w_vt, cos, sin, sg_ln_g, sg_ln_b, sg_w, sg_bias, conv_w, conv_b, conv_ln_g, conv_ln_b]
    out_shape = [jax.ShapeDtypeStruct((SEQ, ATT_WIDTH), BF16),
                 jax.ShapeDtypeStruct((ATT_HEADS, SEQ, LANES), BF16),
                 jax.ShapeDtypeStruct((ATT_HEADS, VT_ROWS, SEQ), BF16),
                 jax.ShapeDtypeStruct((N_BLOCKS, SUBLANES, ATT_WIDTH), F32),
                 jax.ShapeDtypeStruct((SEQ, SG_WIDTH), BF16),
                 jax.ShapeDtypeStruct((SEQ, CONV_WIDTH), BF16)]
    out_specs = [row(ATT_WIDTH), pl.BlockSpec((ATT_HEADS, ROW_TILE, LANES), lambda i: (0, i, 0)),
                 pl.BlockSpec((ATT_HEADS, VT_ROWS, ROW_TILE), lambda i: (0, 0, i)),
                 pl.BlockSpec((1, SUBLANES, ATT_WIDTH), lambda i: (i, 0, 0)),
                 row(SG_WIDTH), row(CONV_WIDTH)]
    if apply_ln:
        out_shape = [jax.ShapeDtypeStruct((SEQ, D_MODEL), F32)] + out_shape
        out_specs = [row(D_MODEL)] + out_specs
    return pl.pallas_call(
        functools.partial(_front_kernel, apply_ln),
        grid=(n_tiles,),
        in_specs=in_specs,
        out_specs=out_specs,
        out_shape=out_shape,
        scratch_shapes=[pltpu.VMEM((CONV_HALO + ROW_TILE, CONV_WIDTH), F32)],
        compiler_params=pltpu.CompilerParams(dimension_semantics=("arbitrary",), vmem_limit_bytes=VMEM_LIMIT),
        name="front_ln" if apply_ln else "front",
    )(*args)


def _attn_kernel(q_ref, kx_ref, vt_ref, km_ref, o_ref, s0_ref, s1_ref, mx0_ref, mx1_ref):
    first_blk = pl.program_id(1) * ATT_Q_BLOCKS
    q = q_ref[...]

    km = km_ref[...]
    km_lane = lax.broadcasted_iota(jnp.int32, (N_BLOCKS, LANES), 1)
    km_rows = jnp.concatenate([jnp.where(km_lane >= HEAD_DIM, km, 0.0), jnp.where(km_lane < HEAD_DIM, km, 0.0)],
                              axis=0)
    km_hi, km_lo = _split_bf16(km_rows)
    eye = (lax.broadcasted_iota(jnp.int32, (LANES, LANES), 0)
           == lax.broadcasted_iota(jnp.int32, (LANES, LANES), 1)).astype(F32).astype(BF16)
    res = _dot_nt(jnp.concatenate([km_hi, km_lo, eye], axis=0), q)
    q_t = res[2 * LANES:].astype(BF16)
    gate = (res[0:LANES] + res[LANES:2 * LANES]).reshape(2, N_BLOCKS, ATT_Q_TILE)
    blk = lax.broadcasted_iota(jnp.int32, (2, N_BLOCKS, ATT_Q_TILE), 1)
    own = first_blk + lax.broadcasted_iota(jnp.int32, (2, N_BLOCKS, ATT_Q_TILE), 2) // MOBA_BLOCK
    gate = jnp.where(blk < own, gate, NEG)

    sel = jnp.zeros((2, N_BLOCKS, ATT_Q_TILE), jnp.bool_)
    blk_f = blk.astype(F32)
    for r in range(MOBA_TOPK):
        best = jnp.max(gate, axis=1, keepdims=True)
        first = jnp.min(jnp.where(gate == best, blk_f, float(N_BLOCKS)), axis=1, keepdims=True)
        pick = blk_f == first
        sel = sel | (pick & (own > r))
        gate = jnp.where(pick, -jnp.inf, gate)
    bias_t = jnp.where(sel, 0.0, NEG).astype(BF16).reshape(LANES, ATT_Q_TILE)
    low = lax.broadcasted_iota(jnp.int32, (LANES, ATT_Q_TILE), 0) < HEAD_DIM
    zero = jnp.zeros_like(q_t)
    qx_past = (jnp.where(low, q_t, bias_t), jnp.where(low, bias_t, q_t))
    qx_own = (jnp.where(low, q_t, zero), jnp.where(low, zero, q_t))

    key_i = lax.broadcasted_iota(jnp.int32, (ATT_Q_TILE, ATT_Q_TILE), 0)
    qry_i = lax.broadcasted_iota(jnp.int32, (ATT_Q_TILE, ATT_Q_TILE), 1)
    own_mask = (key_i <= qry_i) & (key_i // MOBA_BLOCK == qry_i // MOBA_BLOCK)
    start_own = pl.multiple_of(first_blk * MOBA_BLOCK, ATT_Q_TILE)

    state = []
    for par in range(2):
        s = _dot(kx_ref[par, pl.ds(start_own, ATT_Q_TILE), :], qx_own[par])
        s = jnp.where(own_mask, s, NEG)
        m = jnp.max(s, axis=0, keepdims=True)
        p = jnp.exp2(s - m)
        acc = _dot(vt_ref[par, :, pl.ds(start_own, ATT_Q_TILE)], p.astype(BF16))
        state += [m, acc]

    n_groups = lax.shift_right_logical(first_blk + (ATT_Q_BLOCKS - 1) + (ATT_GROUP - 1), ATT_GROUP_SHIFT)

    def scores(g, s_ref, mx_ref):
        start = pl.multiple_of(g * ATT_GROUP_KEYS, ATT_GROUP_KEYS)
        for par in range(2):
            s = _dot(kx_ref[par, pl.ds(start, ATT_GROUP_KEYS), :], qx_past[par])
            s_ref[par] = s
            mx_ref[par] = jnp.max(s, axis=0, keepdims=True)

    def consume(g, s_ref, mx_ref, carry):
        start = pl.multiple_of(g * ATT_GROUP_KEYS, ATT_GROUP_KEYS)
        out = []
        for par in range(2):
            m, acc = carry[2 * par:2 * par + 2]
            m_new = jnp.maximum(m, mx_ref[par])
            a = jnp.exp2(m - m_new)
            p = jnp.exp2(s_ref[par] - m_new)
            acc = a * acc + _dot(vt_ref[par, :, pl.ds(start, ATT_GROUP_KEYS)], p.astype(BF16))
            out += [m_new, acc]
        return tuple(out)

    scores(0, s0_ref, mx0_ref)

    def body(g, carry):
        def even(c):
            scores(g + 1, s1_ref, mx1_ref)
            return consume(g, s0_ref, mx0_ref, c)

        def odd(c):
            scores(g + 1, s0_ref, mx0_ref)
            return consume(g, s1_ref, mx1_ref, c)

        return lax.cond((g & 1) == 0, even, odd, carry)

    last = n_groups - 1
    state = lax.fori_loop(0, last, body, tuple(state))
    state = lax.cond((last & 1) == 0, lambda c: consume(last, s0_ref, mx0_ref, c),
                     lambda c: consume(last, s1_ref, mx1_ref, c), state)
    o_t = jnp.concatenate([acc[0:HEAD_DIM] / acc[HEAD_DIM:HEAD_DIM + 1] for acc in (state[1], state[3])],
                          axis=0)
    o_ref[...] = o_t.T.astype(BF16)


def _attn(q, kx, vt, kmean):
    n_pairs = ATT_WIDTH // LANES
    return pl.pallas_call(
        _attn_kernel,
        grid=(n_pairs, SEQ // ATT_Q_TILE),
        in_specs=[pl.BlockSpec((ATT_Q_TILE, LANES), lambda j, t: (t, j)),
                  pl.BlockSpec((2, SEQ, LANES), lambda j, t: (j, 0, 0)),
                  pl.BlockSpec((2, VT_ROWS, SEQ), lambda j, t: (j, 0, 0)),
                  pl.BlockSpec((N_BLOCKS, LANES), lambda j, t: (0, j))],
        out_specs=pl.BlockSpec((ATT_Q_TILE, LANES), lambda j, t: (t, j)),
        out_shape=jax.ShapeDtypeStruct((SEQ, ATT_WIDTH), BF16),
        scratch_shapes=([pltpu.VMEM((2, ATT_GROUP_KEYS, ATT_Q_TILE), F32)] * 2
                        + [pltpu.VMEM((2, 1, ATT_Q_TILE), F32)] * 2),
        compiler_params=pltpu.CompilerParams(dimension_semantics=("arbitrary", "arbitrary"),
                                             vmem_limit_bytes=VMEM_LIMIT),
        name="attn",
    )(q, kx, vt, kmean)


def _outproj_kernel(h_ref, ya_ref, ys_ref, yc_ref, wo_ref, g_ref, b_ref, wr_ref, rb_ref,
                    h1x_ref, grp_ref, rank_ref, cnt_ref):
    mix = (_dot(ya_ref[...], wo_ref[0:ATT_WIDTH, :])
           + _dot(ys_ref[...], wo_ref[ATT_WIDTH:ATT_WIDTH + SG_WIDTH, :])
           + _dot(yc_ref[...], wo_ref[ATT_WIDTH + SG_WIDTH:, :]))
    h1 = _layer_norm(DEEPNORM_ALPHA * h_ref[...] + mix, g_ref[...], b_ref[...])
    h1x_ref[:, 0:D_MODEL] = h1

    w_hi, w_lo = _split_bf16(wr_ref[...])
    h_hi, h_lo = _split_bf16(h1)
    both = _dot_nt(jnp.concatenate([w_hi, w_lo], axis=0), h_hi)
    logits = both[0:N_EXPERTS] + both[N_EXPERTS:] + _dot_nt(w_hi, h_lo)
    scores = jax.nn.sigmoid(logits)
    biased = scores + rb_ref[...]
    sc = [scores[e:e + 1, :] for e in range(N_EXPERTS)]
    bi = [biased[e:e + 1, :] for e in range(N_EXPERTS)]

    best = None
    for g in range(N_EXPERT_GROUPS):
        a0, a1, a2, a3 = bi[4 * g:4 * g + 4]
        hi01, lo01 = jnp.maximum(a0, a1), jnp.minimum(a0, a1)
        hi23, lo23 = jnp.maximum(a2, a3), jnp.minimum(a2, a3)
        top1 = jnp.maximum(hi01, hi23)
        top2 = jnp.maximum(jnp.minimum(hi01, hi23), jnp.maximum(lo01, lo23))
        gs = top1 + top2
        if best is None:
            best, grp = gs, jnp.zeros_like(gs, dtype=jnp.int32)
        else:
            upd = gs > best
            best = jnp.where(upd, gs, best)
            grp = jnp.where(upd, g, grp)

    def in_group(vals, k):
        out = vals[k]
        for g in range(1, N_EXPERT_GROUPS):
            out = jnp.where(grp == g, vals[4 * g + k], out)
        return out

    vb = [in_group(bi, k) for k in range(EXPERTS_PER_GROUP)]
    vs = [in_group(sc, k) for k in range(EXPERTS_PER_GROUP)]
    m1, i1, s1 = vb[0], jnp.zeros_like(grp), vs[0]
    for k in range(1, EXPERTS_PER_GROUP):
        upd = vb[k] > m1
        m1 = jnp.where(upd, vb[k], m1)
        i1 = jnp.where(upd, k, i1)
        s1 = jnp.where(upd, vs[k], s1)
    m2 = jnp.full_like(m1, -jnp.inf)
    i2 = jnp.zeros_like(grp)
    s2 = jnp.zeros_like(s1)
    for k in range(EXPERTS_PER_GROUP):
        upd = (i1 != k) & (vb[k] > m2)
        m2 = jnp.where(upd, vb[k], m2)
        i2 = jnp.where(upd, k, i2)
        s2 = jnp.where(upd, vs[k], s2)
    tot = s1 + s2
    row = lax.broadcasted_iota(jnp.int32, (GATE_ROWS, ROW_TILE), 0)
    rest = (jnp.where(row % EXPERTS_PER_GROUP == i1, s1 / tot, 0.0)
            + jnp.where(row % EXPERTS_PER_GROUP == i2, s2 / tot, 0.0))
    pieces = jnp.zeros((GATE_ROWS, ROW_TILE), F32)
    for i in range(GATE_PIECES):
        piece = rest.astype(BF16).astype(F32)
        pieces = jnp.where(row // EXPERTS_PER_GROUP == i, piece, pieces)
        rest = rest - piece
    t_r = lax.broadcasted_iota(jnp.int32, (ROW_TILE, ROW_TILE), 0)
    t_c = lax.broadcasted_iota(jnp.int32, (ROW_TILE, ROW_TILE), 1)
    pieces_t = _dot_nt(jnp.where(t_r == t_c, 1.0, 0.0).astype(BF16), pieces.astype(BF16))
    p_r = lax.broadcasted_iota(jnp.int32, (GATE_ROWS, LANES), 0)
    p_c = lax.broadcasted_iota(jnp.int32, (GATE_ROWS, LANES), 1)
    add_pieces = jnp.where((p_r % EXPERTS_PER_GROUP == p_c) & (p_r < GATE_PIECES * EXPERTS_PER_GROUP), 1.0, 0.0)
    h1x_ref[:, D_MODEL:] = _dot(pieces_t.astype(BF16), add_pieces.astype(BF16))
    grp_ref[...] = grp
    k_idx = lax.broadcasted_iota(jnp.int32, (EXPERTS_PER_GROUP, ROW_TILE), 0)

    @pl.when(pl.program_id(0) == 0)
    def _():
        cnt_ref[...] = jnp.zeros_like(cnt_ref)

    member = (k_idx == grp).astype(F32)
    earlier = _dot(member.astype(BF16), (t_r < t_c).astype(BF16)) + cnt_ref[...]
    rank_ref[...] = jnp.sum(member * earlier, axis=0, keepdims=True).astype(jnp.int32)
    cnt_ref[...] += jnp.sum(member, axis=1, keepdims=True)


def _outproj(h, y_att, y_sg, y_cv, w_out, ln_g, ln_b, w_router_t, router_bias):
    row = lambda w: pl.BlockSpec((ROW_TILE, w), lambda i: (i, 0))
    full = lambda shape: pl.BlockSpec(shape, lambda i: (0,) * len(shape))
    lane_row = lambda n: pl.BlockSpec((n, ROW_TILE), lambda i: (0, i))
    return pl.pallas_call(
        _outproj_kernel,
        grid=(SEQ // ROW_TILE,),
        in_specs=[row(D_MODEL), row(ATT_WIDTH), row(SG_WIDTH), row(CONV_WIDTH), full((D_MODEL, D_MODEL)),
                  full((1, D_MODEL)), full((1, D_MODEL)), full((N_EXPERTS, D_MODEL)), full((N_EXPERTS, 1))],
        out_specs=[row(MOE_ROW_WIDTH), lane_row(1), lane_row(1)],
        out_shape=[jax.ShapeDtypeStruct((SEQ, MOE_ROW_WIDTH), F32),
                   jax.ShapeDtypeStruct((1, SEQ), jnp.int32),
                   jax.ShapeDtypeStruct((1, SEQ), jnp.int32)],
        scratch_shapes=[pltpu.VMEM((N_EXPERT_GROUPS, ROW_TILE), F32)],
        compiler_params=pltpu.CompilerParams(dimension_semantics=("arbitrary",), vmem_limit_bytes=VMEM_LIMIT),
        name="outproj",
    )(h, y_att, y_sg, y_cv, w_out, ln_g, ln_b, w_router_t, router_bias)


def _moe_plan(grp, rank):
    member = (grp[:, None] == jnp.arange(N_EXPERT_GROUPS)[None, :]).astype(jnp.int32)
    tiles = (member.sum(axis=0) + MOE_TILE - 1) // MOE_TILE
    tile0 = jnp.cumsum(tiles) - tiles
    pos = tile0[grp] * MOE_TILE + rank
    src = jnp.zeros((MOE_SORTED_ROWS,), jnp.int32).at[pos].set(jnp.arange(SEQ, dtype=jnp.int32))
    n_tiles = tiles.sum()
    tile_idx = jnp.minimum(jnp.arange(MOE_MAX_TILES), n_tiles - 1)
    tile_grp = (tile_idx[:, None] >= tile0[None, 1:]).sum(axis=1)
    return pos, src, tile_grp.astype(jnp.int32), n_tiles.reshape(1).astype(jnp.int32)


def _row_copy(src_hbm, buf, sem, slot, src_row, dst_row, n_rows):
    return pltpu.make_async_copy(src_hbm.at[pl.ds(src_row, n_rows), :], buf.at[slot, pl.ds(dst_row, n_rows), :],
                                 sem.at[slot])


def _start_row_gather(idx_ref, src_hbm, buf, sem, tile, slot, n_rows):
    def body(k, carry):
        for priority in range(2):
            r = 2 * k + priority
            _row_copy(src_hbm, buf, sem, slot, idx_ref[tile * n_rows + r], r, 1).start(priority=priority)
        return carry
    lax.fori_loop(0, n_rows // 2, body, 0, unroll=8)


def _moe_ffn_kernel(src_ref, grp_ref, nt_ref, hx_hbm, wg_ref, wu_ref, wd_ref, y_ref, buf, sem):
    i = pl.program_id(0)
    n_tiles = nt_ref[0]
    slot = i & 1

    @pl.when(i == 0)
    def _():
        _start_row_gather(src_ref, hx_hbm, buf, sem, 0, 0, MOE_TILE)

    @pl.when(i + 1 < n_tiles)
    def _():
        _start_row_gather(src_ref, hx_hbm, buf, sem, i + 1, 1 - slot, MOE_TILE)

    @pl.when(i < n_tiles)
    def _():
        _row_copy(hx_hbm, buf, sem, slot, 0, 0, MOE_TILE).wait()
        x = buf[slot, :, 0:D_MODEL].astype(BF16)
        w = buf[slot, :, D_MODEL:]
        y = jnp.zeros((MOE_TILE, D_MODEL), F32)
        for k in range(EXPERTS_PER_GROUP):
            a = jax.nn.silu(_dot(x, wg_ref[0, k])) * _dot(x, wu_ref[0, k])
            y = y + w[:, k:k + 1] * _dot(a.astype(BF16), wd_ref[0, k])
        y_ref[...] = y

    @pl.when(i >= n_tiles)
    def _():
        y_ref[...] = jnp.zeros_like(y_ref)


def _moe_ffn(layer, src, tile_grp, n_tiles, hx, w_gate, w_up, w_down):
    experts = lambda a, b: pl.BlockSpec((1, EXPERTS_PER_GROUP, a, b),
                                        lambda i, src, grp, nt: (layer, grp[i], 0, 0))
    return pl.pallas_call(
        _moe_ffn_kernel,
        grid_spec=pltpu.PrefetchScalarGridSpec(
            num_scalar_prefetch=3,
            grid=(MOE_MAX_TILES,),
            in_specs=[pl.BlockSpec(memory_space=pl.ANY),
                      experts(D_MODEL, D_EXPERT), experts(D_MODEL, D_EXPERT), experts(D_EXPERT, D_MODEL)],
            out_specs=pl.BlockSpec((MOE_TILE, D_MODEL), lambda i, src, grp, nt: (i, 0)),
            scratch_shapes=[pltpu.VMEM((2, MOE_TILE, MOE_ROW_WIDTH), F32), pltpu.SemaphoreType.DMA((2,))]),
        out_shape=jax.ShapeDtypeStruct((MOE_SORTED_ROWS, D_MODEL), F32),
        compiler_params=pltpu.CompilerParams(dimension_semantics=("arbitrary",), vmem_limit_bytes=VMEM_LIMIT),
        name="moe_ffn",
    )(src, tile_grp, n_tiles, hx, w_gate, w_up, w_down)


def _moe_combine_kernel(pos_ref, y_hbm, h_ref, g_ref, b_ref, o_ref, buf, sem):
    i = pl.program_id(0)
    slot = i & 1

    @pl.when(i == 0)
    def _():
        _start_row_gather(pos_ref, y_hbm, buf, sem, 0, 0, MOE_SRC_TILE)

    @pl.when(i + 1 < pl.num_programs(0))
    def _():
        _start_row_gather(pos_ref, y_hbm, buf, sem, i + 1, 1 - slot, MOE_SRC_TILE)

    _row_copy(y_hbm, buf, sem, slot, 0, 0, MOE_SRC_TILE).wait()
    o_ref[...] = _layer_norm(DEEPNORM_ALPHA * h_ref[...] + buf[slot], g_ref[...], b_ref[...])


def _moe_combine(y_sorted, pos, h, ln_g, ln_b):
    src_rows = lambda w: pl.BlockSpec((MOE_SRC_TILE, w), lambda i, pos: (i, 0))
    const = lambda shape: pl.BlockSpec(shape, lambda i, pos: (0,) * len(shape))
    return pl.pallas_call(
        _moe_combine_kernel,
        grid_spec=pltpu.PrefetchScalarGridSpec(
            num_scalar_prefetch=1,
            grid=(SEQ // MOE_SRC_TILE,),
            in_specs=[pl.BlockSpec(memory_space=pl.ANY), src_rows(D_MODEL), const((1, D_MODEL)),
                      const((1, D_MODEL))],
            out_specs=src_rows(D_MODEL),
            scratch_shapes=[pltpu.VMEM((2, MOE_SRC_TILE, D_MODEL), F32), pltpu.SemaphoreType.DMA((2,))]),
        out_shape=jax.ShapeDtypeStruct((SEQ, D_MODEL), F32),
        compiler_params=pltpu.CompilerParams(dimension_semantics=("arbitrary",), vmem_limit_bytes=VMEM_LIMIT),
        name="moe_combine",
    )(pos, y_sorted, h, ln_g, ln_b)


def _rope_tables():
    half = HEAD_DIM // 2
    inv_freq = ROPE_THETA ** (-jnp.arange(half, dtype=F32) / half)
    ang = jnp.arange(SEQ, dtype=F32)[:, None] * inv_freq[None, :]
    cos, sin = jnp.cos(ang), jnp.sin(ang)
    reps = LANES // HEAD_DIM
    return jnp.tile(jnp.concatenate([cos, cos], axis=1), (1, reps)), jnp.tile(jnp.concatenate([-sin, sin], axis=1), (1, reps))


def kernel(x, ln_in_g, ln_in_b, w_in, w_out, sg_ln_g, sg_ln_b, sg_w, sg_b, conv_w, conv_b, conv_ln_g, conv_ln_b, ln_mix_g, ln_mix_b, w_router, router_bias, w_gate, w_up, w_down, ln_ffn_g, ln_ffn_b):
    assert x.shape == (1, SEQ, D_MODEL) and w_in.shape == (DEPTH, D_MODEL, IN_COLS)
    cos, sin = _rope_tables()
    row = lambda a: a.reshape(1, -1)
    w_router_t = w_router.T
    rb = router_bias.reshape(N_EXPERTS, 1)
    w_gate_bf16, w_up_bf16, w_down_bf16 = w_gate.astype(BF16), w_up.astype(BF16), w_down.astype(BF16)
    h = x.reshape(SEQ, D_MODEL)
    for l in range(DEPTH):
        sg_bias = jnp.repeat(sg_b[l].T, SG_GROUP_DIM, axis=1)
        w_p = jnp.concatenate([w_in[l][:, :V_OFF], w_in[l][:, SGU_OFF:]], axis=1).astype(BF16)
        w_vt = w_in[l][:, V_OFF:SGU_OFF].T.astype(BF16)
        args = (w_p, w_vt, cos, sin, row(sg_ln_g[l]), row(sg_ln_b[l]), sg_w[l], sg_bias,
                conv_w[l], row(conv_b[l]), row(conv_ln_g[l]), row(conv_ln_b[l]))
        if l == 0:
            h, q, kx, vt, kmean, y_sg, y_cv = _front(True, h, row(ln_in_g), row(ln_in_b), *args)
        else:
            q, kx, vt, kmean, y_sg, y_cv = _front(False, h, None, None, *args)
        y_att = _attn(q, kx, vt, kmean[:, 0, :])
        hx, grp, rank = _outproj(h, y_att, y_sg, y_cv, w_out[l].astype(BF16), row(ln_mix_g[l]),
                                 row(ln_mix_b[l]), w_router_t, rb)
        pos, src, tile_grp, n_tiles = _moe_plan(grp[0], rank[0])
        y_sorted = _moe_ffn(l, src, tile_grp, n_tiles, hx, w_gate_bf16, w_up_bf16, w_down_bf16)
        h = _moe_combine(y_sorted, pos, hx, row(ln_ffn_g[l]), row(ln_ffn_b[l]))
    return h.reshape(1, SEQ, D_MODEL)
```
